```python
import math
import jax, jax.numpy as jnp
from jax import lax
import numpy as np

D_MODEL = 2048
BATCH = 1
SEQ = 16384
DEPTH = 4

N_A_LAYERS = DEPTH // 2
N_B_LAYERS = DEPTH - N_A_LAYERS
CHUNK = 128
A_WIDTH = D_MODEL
A_GROUPS = 8
A_GROUP_DIM = A_WIDTH // A_GROUPS
HEAD_DIM = 64
N_Q_HEADS = D_MODEL // HEAD_DIM
N_KV_HEADS = 4
Q_PER_KV = N_Q_HEADS // N_KV_HEADS
WINDOW = 128
ROPE_THETA = 10000.0
D_FF = 4 * D_MODEL
LN_EPS = 1e-5
DEEPNORM_ALPHA = (2.0 * DEPTH) ** 0.25
DEEPNORM_BETA = (8.0 * DEPTH) ** -0.25

kernel_name = "yoco_sgu_swa_sink_deepnorm_trunk"


def layer_norm(x, g, b):
    xf = x.astype(jnp.float32)
    mu = jnp.mean(xf, axis=-1, keepdims=True)
    var = jnp.mean(jnp.square(xf - mu), axis=-1, keepdims=True)
    y = (xf - mu) * lax.rsqrt(var + LN_EPS) * g.astype(jnp.float32) + b.astype(jnp.float32)
    return y.astype(x.dtype)


def rope(t, positions):
    hd = t.shape[-1]
    inv_freq = ROPE_THETA ** (-jnp.arange(0, hd, 2, dtype=jnp.float32) / hd)
    ang = positions.astype(jnp.float32)[:, None] * inv_freq[None, :]
    cos = jnp.cos(ang)[None, :, None, :]
    sin = jnp.sin(ang)[None, :, None, :]
    tf = t.astype(jnp.float32)
    t1, t2 = tf[..., : hd // 2], tf[..., hd // 2:]
    out = jnp.concatenate([t1 * cos - t2 * sin, t2 * cos + t1 * sin], axis=-1)
    return out.astype(t.dtype)


def chunked_sgu(x, w_in, b_in, ln_v_g, ln_v_b, w_s, b_s, w_out):
    B, S, _ = x.shape
    nc = S // CHUNK
    z = jax.nn.gelu(x @ w_in + b_in, approximate=False)
    u, v = jnp.split(z, 2, axis=-1)
    v = layer_norm(v, ln_v_g, ln_v_b)
    v = v.reshape(B, nc, CHUNK, A_GROUPS, A_GROUP_DIM)
    causal = jnp.tril(jnp.ones((CHUNK, CHUNK), dtype=w_s.dtype))
    ws = w_s * causal[None]
    s = jnp.einsum('gij,bnjgd->bnigd', ws, v) + b_s.T[None, None, :, :, None]
    y = u * s.reshape(B, S, A_WIDTH)
    return y @ w_out


def shared_kv_bands(x_kv, w_kv, b_kv, positions):
    B, S, _ = x_kv.shape
    nb = S // WINDOW
    kv = x_kv @ w_kv + b_kv
    k, v = jnp.split(kv, 2, axis=-1)
    k = rope(k.reshape(B, S, N_KV_HEADS, HEAD_DIM), positions)
    v = v.reshape(B, S, N_KV_HEADS, HEAD_DIM)

    def band(t):
        blk = t.reshape(B, nb, WINDOW, N_KV_HEADS, HEAD_DIM)
        prev = jnp.pad(blk[:, :-1], ((0, 0), (1, 0), (0, 0), (0, 0), (0, 0)))
        return jnp.concatenate([prev, blk], axis=2)

    return band(k), band(v)


def sliding_sink_attention(x, k_band, v_band, w_q, b_q, sinks, w_o, positions):
    B, S, _ = x.shape
    nb = S // WINDOW
    q = rope((x @ w_q + b_q).reshape(B, S, N_Q_HEADS, HEAD_DIM), positions)
    q = q.reshape(B, nb, WINDOW, N_KV_HEADS, Q_PER_KV, HEAD_DIM)
    scores = jnp.einsum('bnqhgd,bnkhd->bnhgqk', q, k_band).astype(jnp.float32)
    scores = scores * (HEAD_DIM ** -0.5)
    i = jnp.arange(WINDOW)[:, None]
    j = jnp.arange(2 * WINDOW)[None, :]
    in_band = (j > i) & (j <= i + WINDOW)
    blk = jnp.arange(nb)[:, None, None]
    valid = in_band[None] & ((blk > 0) | (j[None] >= WINDOW))
    scores = jnp.where(valid[None, :, None, None], scores, jnp.finfo(jnp.float32).min)
    sink = jnp.broadcast_to(
        sinks.astype(jnp.float32).reshape(1, 1, N_KV_HEADS, Q_PER_KV, 1, 1),
        scores.shape[:-1] + (1,))
    p = jax.nn.softmax(jnp.concatenate([scores, sink], axis=-1), axis=-1)[..., :-1]
    o = jnp.einsum('bnhgqk,bnkhd->bnqhgd', p.astype(v_band.dtype), v_band)
    return o.reshape(B, S, N_Q_HEADS * HEAD_DIM) @ w_o


def sq_relu_mlp(x, w_up, w_down):
    return jnp.square(jax.nn.relu(x @ w_up)) @ w_down


def setup_inputs(seed: int = 0) -> dict:
    key = jax.random.key(seed)
    ks = jax.random.split(key, 20)
    f32 = jnp.float32

    def nrm(k, shape, scale):
        return jax.random.normal(k, shape, f32) * scale

    x = jax.random.normal(ks[0], (BATCH, SEQ, D_MODEL), f32)
    a_w_in = nrm(ks[1], (N_A_LAYERS, D_MODEL, 2 * A_WIDTH), D_MODEL ** -0.5)
    a_b_in = nrm(ks[2], (N_A_LAYERS, 2 * A_WIDTH), 0.02)
    a_ln_v_g = 1.0 + nrm(ks[3], (N_A_LAYERS, A_WIDTH), 0.02)
    a_ln_v_b = nrm(ks[4], (N_A_LAYERS, A_WIDTH), 0.02)
    a_w_s = nrm(ks[5], (N_A_LAYERS, A_GROUPS, CHUNK, CHUNK), CHUNK ** -0.5)
    a_b_s = 1.0 + nrm(ks[6], (N_A_LAYERS, A_GROUPS, CHUNK), 0.1)
    a_w_out = nrm(ks[7], (N_A_LAYERS, A_WIDTH, D_MODEL), DEEPNORM_BETA * A_WIDTH ** -0.5)
    w_k = nrm(ks[8], (D_MODEL, N_KV_HEADS * HEAD_DIM), D_MODEL ** -0.5)
    w_v = nrm(ks[9], (D_MODEL, N_KV_HEADS * HEAD_DIM), DEEPNORM_BETA * D_MODEL ** -0.5)
    kv_w = jnp.concatenate([w_k, w_v], axis=-1)
    kv_b = nrm(ks[10], (2 * N_KV_HEADS * HEAD_DIM,), 0.02)
    b_w_q = nrm(ks[11], (N_B_LAYERS, D_MODEL, N_Q_HEADS * HEAD_DIM), D_MODEL ** -0.5)
    b_b_q = nrm(ks[12], (N_B_LAYERS, N_Q_HEADS * HEAD_DIM), 0.02)
    b_sinks = nrm(ks[13], (N_B_LAYERS, N_Q_HEADS), 1.0)
    b_w_o = nrm(ks[14], (N_B_LAYERS, N_Q_HEADS * HEAD_DIM, D_MODEL),
                DEEPNORM_BETA * (N_Q_HEADS * HEAD_DIM) ** -0.5)
    mlp_w_up = nrm(ks[15], (DEPTH, D_MODEL, D_FF), DEEPNORM_BETA * D_MODEL ** -0.5)
    mlp_w_down = nrm(ks[16], (DEPTH, D_FF, D_MODEL), DEEPNORM_BETA * D_FF ** -0.5)
    ln_g = 1.0 + nrm(ks[17], (DEPTH, 2, D_MODEL), 0.02)
    ln_b = nrm(ks[18], (DEPTH, 2, D_MODEL), 0.02)
    return {"x": x, "a_w_in": a_w_in, "a_b_in": a_b_in, "a_ln_v_g": a_ln_v_g,
            "a_ln_v_b": a_ln_v_b, "a_w_s": a_w_s, "a_b_s": a_b_s, "a_w_out": a_w_out,
            "kv_w": kv_w, "kv_b": kv_b, "b_w_q": b_w_q, "b_b_q": b_b_q,
            "b_sinks": b_sinks, "b_w_o": b_w_o, "mlp_w_up": mlp_w_up,
            "mlp_w_down": mlp_w_down, "ln_g": ln_g, "ln_b": ln_b}


def reference(x, a_w_in, a_b_in, a_ln_v_g, a_ln_v_b, a_w_s, a_b_s, a_w_out,
              kv_w, kv_b, b_w_q, b_b_q, b_sinks, b_w_o, mlp_w_up, mlp_w_down,
              ln_g, ln_b):
    S = x.shape[1]
    positions = jnp.arange(S, dtype=jnp.int32)
    k_band = v_band = None
    for layer in range(DEPTH):
        if layer < N_A_LAYERS:
            mix = chunked_sgu(x, a_w_in[layer], a_b_in[layer], a_ln_v_g[layer],
                              a_ln_v_b[layer], a_w_s[layer], a_b_s[layer], a_w_out[layer])
        else:
            if layer == N_A_LAYERS:
                k_band, v_band = shared_kv_bands(x, kv_w, kv_b, positions)
            j = layer - N_A_LAYERS
            mix = sliding_sink_attention(x, k_band, v_band, b_w_q[j], b_b_q[j],
                                         b_sinks[j], b_w_o[j], positions)
        x = layer_norm(DEEPNORM_ALPHA * x + mix, ln_g[layer, 0], ln_b[layer, 0])
        x = layer_norm(DEEPNORM_ALPHA * x + sq_relu_mlp(x, mlp_w_up[layer], mlp_w_down[layer]),
                       ln_g[layer, 1], ln_b[layer, 1])
    return x
```

```python
import functools

import numpy as np
import jax
import jax.numpy as jnp
from jax import lax
from jax.experimental import pallas as pl
from jax.experimental.pallas import tpu as pltpu

D_MODEL = 2048
DEPTH = 4
N_A_LAYERS = DEPTH // 2
CHUNK = 128
A_WIDTH = D_MODEL
A_GROUPS = 8
A_GROUP_DIM = A_WIDTH // A_GROUPS
HEAD_DIM = 64
HALF = HEAD_DIM // 2
N_Q_HEADS = D_MODEL // HEAD_DIM
N_KV_HEADS = 4
Q_PER_KV = N_Q_HEADS // N_KV_HEADS
PAIRS = Q_PER_KV // 2
WINDOW = 128
BAND = 2 * WINDOW
ROPE_THETA = 10000.0
D_FF = 4 * D_MODEL
LN_EPS = 1e-5
DEEPNORM_ALPHA = (2.0 * DEPTH) ** 0.25
KV_WIDTH = N_KV_HEADS * HEAD_DIM

LANES = 128
V7X_VMEM_LIMIT = 56 * 1024 * 1024

BF16 = jnp.bfloat16
F32 = jnp.float32
NEG = float(np.finfo(np.float32).min)

TM_SGU = 256
TM_PROJ = 512
TM_MLP = 1024
TF_MLP = 512
TM_KV = 512
TM_ATT = 512
ROWS_LN = 256


def _cparams(*sem):
    return pltpu.CompilerParams(dimension_semantics=sem, vmem_limit_bytes=V7X_VMEM_LIMIT)


def _resident(shape):
    nd = len(shape)
    return pl.BlockSpec(shape, lambda *_: (0,) * nd, pipeline_mode=pl.Buffered(1))


def _gelu(t):
    return 0.5 * t * (1.0 + lax.erf(t * np.float32(np.sqrt(0.5))))


def _layer_norm(t, g, b):
    mu = jnp.mean(t, axis=-1, keepdims=True)
    c = t - mu
    var = jnp.mean(c * c, axis=-1, keepdims=True)
    return c * lax.rsqrt(var + LN_EPS) * g + b


def _sgu_kernel(x_ref, win_ref, bin_ref, g_ref, b_ref, ws_ref, bs_ref, y_ref, u_scr, v_scr):
    tm = x_ref.shape[0]
    xb = x_ref[...].astype(BF16)
    ncol = 512
    for c in range(2 * A_WIDTH // ncol):
        cs = slice(c * ncol, (c + 1) * ncol)
        z = jnp.dot(xb, win_ref[:, cs], preferred_element_type=F32) + bin_ref[:, cs]
        z = _gelu(z)
        if c * ncol < A_WIDTH:
            u_scr[:, cs] = z
        else:
            v_scr[:, c * ncol - A_WIDTH:(c + 1) * ncol - A_WIDTH] = z
    v_scr[...] = _layer_norm(v_scr[...], g_ref[...], b_ref[...])
    row = lax.broadcasted_iota(jnp.int32, (CHUNK, CHUNK), 0)
    col = lax.broadcasted_iota(jnp.int32, (CHUNK, CHUNK), 1)
    causal = col <= row
    for g in range(A_GROUPS):
        gs = slice(g * A_GROUP_DIM, (g + 1) * A_GROUP_DIM)
        wsg = jnp.where(causal, ws_ref[g], 0.0).astype(BF16)
        bsg = bs_ref[g]
        for c in range(tm // CHUNK):
            rs = slice(c * CHUNK, (c + 1) * CHUNK)
            s = jnp.dot(wsg, v_scr[rs, gs].astype(BF16), preferred_element_type=F32) + bsg
            y_ref[rs, gs] = (u_scr[rs, gs] * s).astype(BF16)


def _sgu(x, w_in, b_in, g, b, w_s, b_s):
    S = x.shape[0]
    tm = TM_SGU
    return pl.pallas_call(
        _sgu_kernel,
        out_shape=jax.ShapeDtypeStruct((S, A_WIDTH), BF16),
        grid=(S // tm,),
        in_specs=[
            pl.BlockSpec((tm, D_MODEL), lambda i: (i, 0)),
            _resident((D_MODEL, 2 * A_WIDTH)),
            _resident((1, 2 * A_WIDTH)),
            _resident((1, A_WIDTH)),
            _resident((1, A_WIDTH)),
            _resident((A_GROUPS, CHUNK, CHUNK)),
            _resident((A_GROUPS, CHUNK, A_GROUP_DIM)),
        ],
        out_specs=pl.BlockSpec((tm, A_WIDTH), lambda i: (i, 0)),
        scratch_shapes=[pltpu.VMEM((tm, A_WIDTH), F32), pltpu.VMEM((tm, A_WIDTH), F32)],
        compiler_params=_cparams("parallel"),
        name="sgu_mix",
    )(x, w_in, b_in, g, b, w_s, b_s)


def _proj_kernel(y_ref, w_ref, x_ref, g_ref, b_ref, o_ref):
    tm = x_ref.shape[0]
    for r in range(tm // ROWS_LN):
        rs = slice(r * ROWS_LN, (r + 1) * ROWS_LN)
        mix = jnp.dot(y_ref[rs, :], w_ref[...], preferred_element_type=F32)
        o_ref[rs, :] = _layer_norm(DEEPNORM_ALPHA * x_ref[rs, :] + mix, g_ref[...], b_ref[...])


def _proj_res_ln(y, w, x, g, b):
    S = x.shape[0]
    tm = TM_PROJ
    return pl.pallas_call(
        _proj_kernel,
        out_shape=jax.ShapeDtypeStruct((S, D_MODEL), F32),
        grid=(S // tm,),
        in_specs=[
            pl.BlockSpec((tm, y.shape[1]), lambda i: (i, 0)),
            _resident(w.shape),
            pl.BlockSpec((tm, D_MODEL), lambda i: (i, 0)),
            _resident((1, D_MODEL)),
            _resident((1, D_MODEL)),
        ],
        out_specs=pl.BlockSpec((tm, D_MODEL), lambda i: (i, 0)),
        compiler_params=_cparams("parallel"),
        name="proj_res_ln",
    )(y, w, x, g, b)


def _mlp_kernel(x_ref, wu_ref, wd_ref, g_ref, b_ref, o_ref, xb_scr):
    j = pl.program_id(1)

    @pl.when(j == 0)
    def _():
        xb_scr[...] = x_ref[...].astype(BF16)
        o_ref[...] = jnp.zeros_like(o_ref)

    h = jnp.dot(xb_scr[...], wu_ref[...], preferred_element_type=F32)
    h = jnp.square(jnp.maximum(h, 0.0)).astype(BF16)
    o_ref[...] += jnp.dot(h, wd_ref[...], preferred_element_type=F32)

    @pl.when(j == pl.num_programs(1) - 1)
    def _():
        tm = x_ref.shape[0]
        rows = 64

        def body(r, carry):
            rs = pl.ds(pl.multiple_of(r * rows, rows), rows)
            o_ref[rs, :] = _layer_norm(DEEPNORM_ALPHA * x_ref[rs, :] + o_ref[rs, :],
                                       g_ref[...], b_ref[...])
            return carry

        lax.fori_loop(0, tm // rows, body, 0)


def _mlp(x, w_up, w_down, g, b):
    S = x.shape[0]
    tm, tf = TM_MLP, TF_MLP
    return pl.pallas_call(
        _mlp_kernel,
        out_shape=jax.ShapeDtypeStruct((S, D_MODEL), F32),
        grid=(S // tm, D_FF // tf),
        in_specs=[
            pl.BlockSpec((tm, D_MODEL), lambda i, j: (i, 0)),
            pl.BlockSpec((D_MODEL, tf), lambda i, j: (0, j)),
            pl.BlockSpec((tf, D_MODEL), lambda i, j: (j, 0)),
            _resident((1, D_MODEL)),
            _resident((1, D_MODEL)),
        ],
        out_specs=pl.BlockSpec((tm, D_MODEL), lambda i, j: (i, 0)),
        scratch_shapes=[pltpu.VMEM((tm, D_MODEL), BF16)],
        compiler_params=_cparams("parallel", "arbitrary"),
        name="mlp_res_ln",
    )(x, w_up, w_down, g, b)


def _kv_kernel(x_ref, w_ref, b_ref, cos_ref, sin_ref, ek_ref, ev_ref, vone_ref, k_out, v_out):
    xb = x_ref[...].astype(BF16)
    kv = jnp.dot(xb, w_ref[...], preferred_element_type=F32) + b_ref[...]
    k1 = kv[:, :LANES]
    k2 = kv[:, LANES:2 * LANES]
    v = kv[:, KV_WIDTH:]
    c = cos_ref[...]
    s = sin_ref[...]
    kr = jnp.concatenate([k1 * c - k2 * s, k2 * c + k1 * s], axis=1).astype(BF16)
    kp = jnp.dot(kr, ek_ref[...], preferred_element_type=F32)
    vp = jnp.dot(v.astype(BF16), ev_ref[...], preferred_element_type=F32) + vone_ref[...]
    for h in range(N_KV_HEADS):
        for p in range(2):
            k0 = (h * 2 + p) * LANES
            k_out[h, p] = kp[:, k0:k0 + LANES].astype(BF16)
            v0 = (h * 2 + p) * 2 * LANES
            v_out[h, p] = vp[:, v0:v0 + 2 * LANES].astype(BF16)


def _kv_proj(x, w, b, cos_k, sin_k, ek, ev, vone):
    S = x.shape[0]
    tm = TM_KV
    return pl.pallas_call(
        _kv_kernel,
        out_shape=(jax.ShapeDtypeStruct((N_KV_HEADS, 2, S, LANES), BF16),
                   jax.ShapeDtypeStruct((N_KV_HEADS, 2, S, 2 * LANES), BF16)),
        grid=(S // tm,),
        in_specs=[
            pl.BlockSpec((tm, D_MODEL), lambda i: (i, 0)),
            _resident(w.shape),
            _resident(b.shape),
            pl.BlockSpec((tm, LANES), lambda i: (i, 0)),
            pl.BlockSpec((tm, LANES), lambda i: (i, 0)),
            _resident(ek.shape),
            _resident(ev.shape),
            _resident(vone.shape),
        ],
        out_specs=(pl.BlockSpec((N_KV_HEADS, 2, tm, LANES), lambda i: (0, 0, i, 0)),
                   pl.BlockSpec((N_KV_HEADS, 2, tm, 2 * LANES), lambda i: (0, 0, i, 0))),
        compiler_params=_cparams("parallel"),
        name="kv_proj",
    )(x, w, b, cos_k, sin_k, ek, ev, vone)


def _attn_kernel(sink_ref, x_ref, wq_ref, bq_ref, cos_ref, sin_ref, kp_ref, kc_ref, vp_ref,
                 vc_ref, bias_ref, o_ref, q_scr, k_scr, v_scr):
    tm = x_ref.shape[0]
    i = pl.program_id(0)
    xb = x_ref[...].astype(BF16)
    cos = cos_ref[...]
    sin = sin_ref[...]
    ncol = 2 * LANES
    for c in range(D_MODEL // ncol):
        cs = slice(c * ncol, (c + 1) * ncol)
        q = jnp.dot(xb, wq_ref[:, cs], preferred_element_type=F32) + bq_ref[:, cs]
        for t in range(2):
            qt = q[:, t * LANES:(t + 1) * LANES]
            qt = qt * cos + pltpu.roll(qt, LANES // 2, 1) * sin
            q_scr[:, c * ncol + t * LANES:c * ncol + (t + 1) * LANES] = qt.astype(BF16)

    k_scr[:, :, :WINDOW, :] = kp_ref[...]
    k_scr[:, :, WINDOW:, :] = kc_ref[...]
    v_scr[:, :, :WINDOW, :] = vp_ref[...]
    v_scr[:, :, WINDOW:, :] = vc_ref[...]

    lane = lax.broadcasted_iota(jnp.int32, (WINDOW, LANES), 1)
    even_lanes = lane < HEAD_DIM
    first_tile = jnp.where(i == 0, 1, 0)
    for b in range(tm // WINDOW):
        rs = slice(b * WINDOW, (b + 1) * WINDOW)
        band = slice(b * WINDOW, b * WINDOW + BAND)
        bias = bias_ref[first_tile] if b == 0 else bias_ref[0]
        for h in range(N_KV_HEADS):
            c0 = h * Q_PER_KV * HEAD_DIM
            lhs = jnp.concatenate(
                [q_scr[rs, c0 + j * LANES:c0 + (j + 1) * LANES] for j in range(PAIRS)], axis=0)
            kcat = jnp.concatenate([k_scr[h, 0, band, :], k_scr[h, 1, band, :]], axis=0)
            s = lax.dot_general(lhs, kcat, (((1,), (1,)), ((), ())),
                                preferred_element_type=F32)
            p_rows = []
            sink_terms = []
            for j in range(PAIRS):
                p_cols = []
                e_sink = []
                for p in range(2):
                    sub = s[j * WINDOW:(j + 1) * WINDOW, p * BAND:(p + 1) * BAND] + bias
                    sk = sink_ref[h * Q_PER_KV + 2 * j + p]
                    m = jnp.maximum(jnp.max(sub, axis=1, keepdims=True), sk)
                    p_cols.append(jnp.exp(sub - m).astype(BF16))
                    e_sink.append(jnp.exp(sk - m))
                p_rows.append(jnp.concatenate(p_cols, axis=1))
                sink_terms.append(jnp.where(even_lanes, e_sink[0], e_sink[1]))
            pmat = jnp.concatenate(p_rows, axis=0)
            vcat = jnp.concatenate([v_scr[h, 0, band, :], v_scr[h, 1, band, :]], axis=0)
            oa = jnp.dot(pmat, vcat, preferred_element_type=F32)
            for j in range(PAIRS):
                js = slice(j * WINDOW, (j + 1) * WINDOW)
                den = oa[js, LANES:] + sink_terms[j]
                o_ref[rs, c0 + j * LANES:c0 + (j + 1) * LANES] = (oa[js, :LANES] / den).astype(BF16)


def _attention(sinks, x, w_q, b_q, cos_q, sin_q, kmat, vmat, bias):
    S = x.shape[0]
    tm = TM_ATT
    nb = tm // WINDOW
    prev = lambda i, s: (0, 0, jnp.maximum(i * nb - 1, 0), 0)
    cur = lambda i, s: (0, 0, i, 0)
    grid_spec = pltpu.PrefetchScalarGridSpec(
        num_scalar_prefetch=1,
        grid=(S // tm,),
        in_specs=[
            pl.BlockSpec((tm, D_MODEL), lambda i, s: (i, 0)),
            _resident(w_q.shape),
            _resident(b_q.shape),
            pl.BlockSpec((tm, LANES), lambda i, s: (i, 0)),
            pl.BlockSpec((tm, LANES), lambda i, s: (i, 0)),
            pl.BlockSpec((N_KV_HEADS, 2, WINDOW, LANES), prev),
            pl.BlockSpec((N_KV_HEADS, 2, tm, LANES), cur),
            pl.BlockSpec((N_KV_HEADS, 2, WINDOW, 2 * LANES), prev),
            pl.BlockSpec((N_KV_HEADS, 2, tm, 2 * LANES), cur),
            _resident(bias.shape),
        ],
        out_specs=pl.BlockSpec((tm, D_MODEL), lambda i, s: (i, 0)),
        scratch_shapes=[
            pltpu.VMEM((tm, D_MODEL), BF16),
            pltpu.VMEM((N_KV_HEADS, 2, tm + WINDOW, LANES), BF16),
            pltpu.VMEM((N_KV_HEADS, 2, tm + WINDOW, 2 * LANES), BF16),
        ],
    )
    return pl.pallas_call(
        _attn_kernel,
        out_shape=jax.ShapeDtypeStruct((S, D_MODEL), BF16),
        grid_spec=grid_spec,
        compiler_params=_cparams("parallel"),
        name="swa_attention",
    )(sinks, x, w_q, b_q, cos_q, sin_q, kmat, kmat, vmat, vmat, bias)


def _q_perm():
    perm = np.zeros(D_MODEL, np.int32)
    for h in range(N_KV_HEADS):
        for j in range(PAIRS):
            for half in range(2):
                for p in range(2):
                    for d in range(HALF):
                        new = h * Q_PER_KV * HEAD_DIM + j * LANES + half * 64 + p * HALF + d
                        old = (h * Q_PER_KV + 2 * j + p) * HEAD_DIM + half * HALF + d
                        perm[new] = old
    return perm


def _k_perm():
    perm = np.zeros(KV_WIDTH, np.int32)
    for half in range(2):
        for h in range(N_KV_HEADS):
            for d in range(HALF):
                perm[half * LANES + h * HALF + d] = h * HEAD_DIM + half * HALF + d
    return perm


def _placements():
    ek = np.zeros((KV_WIDTH, N_KV_HEADS * 2 * LANES), np.float32)
    ev = np.zeros((KV_WIDTH, N_KV_HEADS * 2 * 2 * LANES), np.float32)
    vone = np.zeros((1, N_KV_HEADS * 2 * 2 * LANES), np.float32)
    for h in range(N_KV_HEADS):
        for p in range(2):
            for half in range(2):
                for d in range(HALF):
                    ek[half * LANES + h * HALF + d,
                       (h * 2 + p) * LANES + half * 64 + p * HALF + d] = 1.0
            for d in range(HEAD_DIM):
                base = (h * 2 + p) * 2 * LANES
                ev[h * HEAD_DIM + d, base + p * HEAD_DIM + d] = 1.0
                vone[0, base + LANES + p * HEAD_DIM + d] = 1.0
    return ek, ev, vone


def _band_bias():
    i = np.arange(WINDOW)[:, None]
    j = np.arange(BAND)[None, :]
    in_band = (j > i) & (j <= i + WINDOW)
    bias = np.zeros((2, WINDOW, BAND), np.float32)
    bias[0] = np.where(in_band, 0.0, NEG)
    bias[1] = np.where(in_band & (j >= WINDOW), 0.0, NEG)
    return bias


def kernel(x, a_w_in, a_b_in, a_ln_v_g, a_ln_v_b, a_w_s, a_b_s, a_w_out, kv_w, kv_b, b_w_q,
           b_b_q, b_sinks, b_w_o, mlp_w_up, mlp_w_down, ln_g, ln_b):
    B, S, _ = x.shape
    assert B == 1 and S % TM_MLP == 0
    h = x.reshape(S, D_MODEL)

    inv_freq = ROPE_THETA ** (-jnp.arange(0, HEAD_DIM, 2, dtype=F32) / HEAD_DIM)
    ang = jnp.arange(S, dtype=jnp.int32).astype(F32)[:, None] * inv_freq[None, :]
    cos = jnp.cos(ang)
    sin = jnp.sin(ang)
    cos_k = jnp.tile(cos, (1, 4))
    sin_k = jnp.tile(sin, (1, 4))
    scale = HEAD_DIM ** -0.5
    cos_q = cos_k * scale
    sin_q = jnp.concatenate([-sin, -sin, sin, sin], axis=1) * scale

    q_perm = _q_perm()
    k_perm = _k_perm()
    ek, ev, vone = _placements()
    ek = jnp.asarray(ek, BF16)
    ev = jnp.asarray(ev, BF16)
    vone = jnp.asarray(vone, F32)
    bias = jnp.asarray(_band_bias(), F32)

    row = lambda v: v.reshape(1, -1)

    for layer in range(DEPTH):
        if layer < N_A_LAYERS:
            bs = jnp.broadcast_to(a_b_s[layer][:, :, None], (A_GROUPS, CHUNK, A_GROUP_DIM))
            y = _sgu(h, a_w_in[layer].astype(BF16), row(a_b_in[layer]), row(a_ln_v_g[layer]),
                     row(a_ln_v_b[layer]), a_w_s[layer], bs)
            w_mix = a_w_out[layer].astype(BF16)
        else:
            if layer == N_A_LAYERS:
                w_kv = jnp.concatenate([kv_w[:, :KV_WIDTH][:, k_perm], kv_w[:, KV_WIDTH:]], axis=1)
                b_kv = jnp.concatenate([kv_b[:KV_WIDTH][k_perm], kv_b[KV_WIDTH:]])
                kmat, vmat = _kv_proj(h, w_kv.astype(BF16), row(b_kv), cos_k, sin_k, ek, ev, vone)
            j = layer - N_A_LAYERS
            y = _attention(b_sinks[j], h, b_w_q[j][:, q_perm].astype(BF16),
                           row(b_b_q[j][q_perm] ), cos_q, sin_q, kmat, vmat, bias)
            w_mix = b_w_o[j].astype(BF16)
        h = _proj_res_ln(y, w_mix, h, row(ln_g[layer, 0]), row(ln_b[layer, 0]))
        h = _mlp(h, mlp_w_up[layer].astype(BF16), mlp_w_down[layer].astype(BF16),
                 row(ln_g[layer, 1]), row(ln_b[layer, 1]))
    return h.reshape(B, S, D_MODEL)
```

```python
import functools

import numpy as np
import jax
import jax.numpy as jnp
from jax import lax
from jax.experimental import pallas as pl
from jax.experimental.pallas import tpu as pltpu

D_MODEL = 2048
DEPTH = 4
N_A_LAYERS = DEPTH // 2
CHUNK = 128
A_WIDTH = D_MODEL
A_GROUPS = 8
A_GROUP_DIM = A_WIDTH // A_GROUPS
HEAD_DIM = 64
HALF = HEAD_DIM // 2
N_Q_HEADS = D_MODEL // HEAD_DIM
N_KV_HEADS = 4
Q_PER_KV = N_Q_HEADS // N_KV_HEADS
PAIRS = Q_PER_KV // 2
WINDOW = 128
BAND = 2 * WINDOW
ROPE_THETA = 10000.0
D_FF = 4 * D_MODEL
LN_EPS = 1e-5
DEEPNORM_ALPHA = (2.0 * DEPTH) ** 0.25
KV_WIDTH = N_KV_HEADS * HEAD_DIM

LANES = 128
V7X_VMEM_LIMIT = 56 * 1024 * 1024

BF16 = jnp.bfloat16
F32 = jnp.float32
NEG = float(np.finfo(np.float32).min)

TM_SGU = 512
TM_PROJ = 1024
TM_MLP = 1024
TF_MLP = 512
TM_KV = 512
TM_ATT = 512
ROWS_LN = 256


def _cparams(*sem):
    return pltpu.CompilerParams(dimension_semantics=sem, vmem_limit_bytes=V7X_VMEM_LIMIT)


def _resident(shape):
    nd = len(shape)
    return pl.BlockSpec(shape, lambda *_: (0,) * nd, pipeline_mode=pl.Buffered(1))


def _layer_slab(tail, layer):
    nd = len(tail)
    return pl.BlockSpec((None,) + tuple(tail), lambda *_: (layer,) + (0,) * nd,
                        pipeline_mode=pl.Buffered(1))


BF16_SUBLANES = 16


def _cast_specs(stacked, layer, nsteps, step_of):
    _, nrows, ncols = stacked.shape
    nblk = nsteps
    while nrows % nblk or (nrows // nblk) % BF16_SUBLANES:
        nblk //= 2
    rb, group = nrows // nblk, nsteps // nblk
    in_spec = pl.BlockSpec((None, rb, ncols), lambda *ids: (layer, step_of(*ids) // group, 0))
    out_spec = pl.BlockSpec((rb, ncols), lambda *ids: (step_of(*ids) // group, 0))
    return in_spec, out_spec, jax.ShapeDtypeStruct((nrows, ncols), BF16)


def _run_casts(srcs, dsts):
    for src, dst in zip(srcs, dsts):
        dst[...] = src[...].astype(BF16)


def _gelu(t):
    return 0.5 * t * (1.0 + lax.erf(t * np.float32(np.sqrt(0.5))))


def _layer_norm(t, g, b):
    mu = jnp.mean(t, axis=-1, keepdims=True)
    c = t - mu
    var = jnp.mean(c * c, axis=-1, keepdims=True)
    return c * lax.rsqrt(var + LN_EPS) * g + b


def _sgu_kernel(n_cast, x_ref, win_ref, bin_ref, g_ref, b_ref, ws_ref, bs_ref, *rest):
    cast_src, y_ref, cast_dst = rest[:n_cast], rest[n_cast], rest[n_cast + 1:2 * n_cast + 1]
    u_scr, v_scr = rest[2 * n_cast + 1:]
    _run_casts(cast_src, cast_dst)
    tm = x_ref.shape[0]
    xb = x_ref[...].astype(BF16)
    ncol = 512
    for c in range(2 * A_WIDTH // ncol):
        cs = slice(c * ncol, (c + 1) * ncol)
        z = jnp.dot(xb, win_ref[:, cs], preferred_element_type=F32) + bin_ref[:, cs]
        z = _gelu(z)
        if c * ncol < A_WIDTH:
            u_scr[:, cs] = z
        else:
            v_scr[:, c * ncol - A_WIDTH:(c + 1) * ncol - A_WIDTH] = z
    v_scr[...] = _layer_norm(v_scr[...], g_ref[...], b_ref[...])
    row = lax.broadcasted_iota(jnp.int32, (CHUNK, CHUNK), 0)
    col = lax.broadcasted_iota(jnp.int32, (CHUNK, CHUNK), 1)
    causal = col <= row
    for g in range(A_GROUPS):
        gs = slice(g * A_GROUP_DIM, (g + 1) * A_GROUP_DIM)
        wsg = jnp.where(causal, ws_ref[g], 0.0).astype(BF16)
        bsg = bs_ref[g]
        for c in range(tm // CHUNK):
            rs = slice(c * CHUNK, (c + 1) * CHUNK)
            s = jnp.dot(wsg, v_scr[rs, gs].astype(BF16), preferred_element_type=F32) + bsg
            y_ref[rs, gs] = (u_scr[rs, gs] * s).astype(BF16)


def _sgu(x, w_in, b_in, g, b, w_s, b_s, layer, casts=()):
    S = x.shape[0]
    tm = TM_SGU
    nsteps = S // tm
    cast_specs = [_cast_specs(a, l, nsteps, lambda i: i) for a, l in casts]
    return pl.pallas_call(
        functools.partial(_sgu_kernel, len(casts)),
        out_shape=[jax.ShapeDtypeStruct((S, A_WIDTH), BF16)] + [c[2] for c in cast_specs],
        grid=(nsteps,),
        in_specs=[
            pl.BlockSpec((tm, D_MODEL), lambda i: (i, 0)),
            _resident(w_in.shape),
            _layer_slab((1, 2 * A_WIDTH), layer),
            _layer_slab((1, A_WIDTH), layer),
            _layer_slab((1, A_WIDTH), layer),
            _layer_slab((A_GROUPS, CHUNK, CHUNK), layer),
            _layer_slab((A_GROUPS, CHUNK, A_GROUP_DIM), layer),
        ] + [c[0] for c in cast_specs],
        out_specs=[pl.BlockSpec((tm, A_WIDTH), lambda i: (i, 0))] + [c[1] for c in cast_specs],
        scratch_shapes=[pltpu.VMEM((tm, A_WIDTH), F32), pltpu.VMEM((tm, A_WIDTH), F32)],
        compiler_params=_cparams("arbitrary"),
        name="sgu_mix",
    )(x, w_in, b_in, g, b, w_s, b_s, *[a for a, _ in casts])


def _proj_kernel(y_ref, w_ref, x_ref, g_ref, b_ref, o_ref):
    tm = x_ref.shape[0]
    for r in range(tm // ROWS_LN):
        rs = slice(r * ROWS_LN, (r + 1) * ROWS_LN)
        mix = jnp.dot(y_ref[rs, :], w_ref[...], preferred_element_type=F32)
        o_ref[rs, :] = _layer_norm(DEEPNORM_ALPHA * x_ref[rs, :] + mix, g_ref[...], b_ref[...])


def _proj_res_ln(y, w, x, g, b, ln_row):
    S = x.shape[0]
    tm = TM_PROJ
    return pl.pallas_call(
        _proj_kernel,
        out_shape=jax.ShapeDtypeStruct((S, D_MODEL), F32),
        grid=(S // tm,),
        in_specs=[
            pl.BlockSpec((tm, y.shape[1]), lambda i: (i, 0)),
            _resident(w.shape),
            pl.BlockSpec((tm, D_MODEL), lambda i: (i, 0)),
            _layer_slab((1, D_MODEL), ln_row),
            _layer_slab((1, D_MODEL), ln_row),
        ],
        out_specs=pl.BlockSpec((tm, D_MODEL), lambda i: (i, 0)),
        compiler_params=_cparams("parallel"),
        name="proj_res_ln",
    )(y, w, x, g, b)


def _mlp_kernel(n_cast, x_ref, wu_ref, wd_ref, g_ref, b_ref, *rest):
    cast_src, o_ref, cast_dst = rest[:n_cast], rest[n_cast], rest[n_cast + 1:2 * n_cast + 1]
    xb_scr = rest[2 * n_cast + 1]
    _run_casts(cast_src, cast_dst)
    j = pl.program_id(1)
    last = pl.num_programs(1) - 1

    def hidden():
        h = jnp.dot(xb_scr[...], wu_ref[...], preferred_element_type=F32)
        return jnp.square(jnp.maximum(h, 0.0)).astype(BF16)

    @pl.when(j == 0)
    def _():
        xb_scr[...] = x_ref[...].astype(BF16)
        o_ref[...] = jnp.dot(hidden(), wd_ref[...], preferred_element_type=F32)

    @pl.when(jnp.logical_and(j > 0, j < last))
    def _():
        o_ref[...] += jnp.dot(hidden(), wd_ref[...], preferred_element_type=F32)

    @pl.when(j == last)
    def _():
        h = hidden()
        for r in range(x_ref.shape[0] // ROWS_LN):
            rs = slice(r * ROWS_LN, (r + 1) * ROWS_LN)
            t = o_ref[rs, :] + jnp.dot(h[rs, :], wd_ref[...], preferred_element_type=F32)
            o_ref[rs, :] = _layer_norm(DEEPNORM_ALPHA * x_ref[rs, :] + t, g_ref[...], b_ref[...])


def _mlp(x, w_up, w_down, g, b, layer, casts=()):
    S = x.shape[0]
    tm, tf = TM_MLP, TF_MLP
    ni, nj = S // tm, D_FF // tf
    cast_specs = [_cast_specs(a, l, ni * nj, lambda i, j: i * nj + j) for a, l in casts]
    return pl.pallas_call(
        functools.partial(_mlp_kernel, len(casts)),
        out_shape=[jax.ShapeDtypeStruct((S, D_MODEL), F32)] + [c[2] for c in cast_specs],
        grid=(ni, nj),
        in_specs=[
            pl.BlockSpec((tm, D_MODEL), lambda i, j: (i, 0)),
            pl.BlockSpec((D_MODEL, tf), lambda i, j: (0, j)),
            pl.BlockSpec((tf, D_MODEL), lambda i, j: (j, 0)),
            _layer_slab((1, D_MODEL), 2 * layer + 1),
            _layer_slab((1, D_MODEL), 2 * layer + 1),
        ] + [c[0] for c in cast_specs],
        out_specs=[pl.BlockSpec((tm, D_MODEL), lambda i, j: (i, 0))] + [c[1] for c in cast_specs],
        scratch_shapes=[pltpu.VMEM((tm, D_MODEL), BF16)],
        compiler_params=_cparams("arbitrary", "arbitrary"),
        name="mlp_res_ln",
    )(x, w_up, w_down, g, b, *[a for a, _ in casts])


def _kv_kernel(x_ref, w_ref, b_ref, cos_ref, sin_ref, ek_ref, ev_ref, vone_ref, k_out, v_out):
    xb = x_ref[...].astype(BF16)
    kv = jnp.dot(xb, w_ref[...], preferred_element_type=F32) + b_ref[...]
    k1 = kv[:, :LANES]
    k2 = kv[:, LANES:2 * LANES]
    v = kv[:, KV_WIDTH:]
    c = cos_ref[...]
    s = sin_ref[...]
    kr = jnp.concatenate([k1 * c - k2 * s, k2 * c + k1 * s], axis=1).astype(BF16)
    kp = jnp.dot(kr, ek_ref[...], preferred_element_type=F32)
    vp = jnp.dot(v.astype(BF16), ev_ref[...], preferred_element_type=F32) + vone_ref[...]
    for h in range(N_KV_HEADS):
        for p in range(2):
            k0 = (h * 2 + p) * LANES
            k_out[h, p] = kp[:, k0:k0 + LANES].astype(BF16)
            v0 = (h * 2 + p) * 2 * LANES
            v_out[h, p] = vp[:, v0:v0 + 2 * LANES].astype(BF16)


def _kv_proj(x, w, b, cos_k, sin_k, ek, ev, vone):
    S = x.shape[0]
    tm = TM_KV
    return pl.pallas_call(
        _kv_kernel,
        out_shape=(jax.ShapeDtypeStruct((N_KV_HEADS, 2, S, LANES), BF16),
                   jax.ShapeDtypeStruct((N_KV_HEADS, 2, S, 2 * LANES), BF16)),
        grid=(S // tm,),
        in_specs=[
            pl.BlockSpec((tm, D_MODEL), lambda i: (i, 0)),
            _resident(w.shape),
            _resident(b.shape),
            pl.BlockSpec((tm, LANES), lambda i: (i, 0)),
            pl.BlockSpec((tm, LANES), lambda i: (i, 0)),
            _resident(ek.shape),
            _resident(ev.shape),
            _resident(vone.shape),
        ],
        out_specs=(pl.BlockSpec((N_KV_HEADS, 2, tm, LANES), lambda i: (0, 0, i, 0)),
                   pl.BlockSpec((N_KV_HEADS, 2, tm, 2 * LANES), lambda i: (0, 0, i, 0))),
        compiler_params=_cparams("parallel"),
        name="kv_proj",
    )(x, w, b, cos_k, sin_k, ek, ev, vone)


def _attn_kernel(sink_ref, x_ref, wq_ref, bq_ref, cos_ref, sin_ref, kp_ref, kc_ref, vp_ref,
                 vc_ref, bias_ref, o_ref, q_scr, k_scr, v_scr):
    tm = x_ref.shape[0]
    i = pl.program_id(0)
    xb = x_ref[...].astype(BF16)
    cos = cos_ref[...]
    sin = sin_ref[...]
    ncol = 2 * LANES
    for c in range(D_MODEL // ncol):
        cs = slice(c * ncol, (c + 1) * ncol)
        q = jnp.dot(xb, wq_ref[:, cs], preferred_element_type=F32) + bq_ref[:, cs]
        for t in range(2):
            qt = q[:, t * LANES:(t + 1) * LANES]
            qt = qt * cos + pltpu.roll(qt, LANES // 2, 1) * sin
            q_scr[:, c * ncol + t * LANES:c * ncol + (t + 1) * LANES] = qt.astype(BF16)

    k_scr[:, :, :WINDOW, :] = kp_ref[...]
    k_scr[:, :, WINDOW:, :] = kc_ref[...]
    v_scr[:, :, :WINDOW, :] = vp_ref[...]
    v_scr[:, :, WINDOW:, :] = vc_ref[...]

    lane = lax.broadcasted_iota(jnp.int32, (WINDOW, LANES), 1)
    even_lanes = lane < HEAD_DIM
    first_tile = jnp.where(i == 0, 1, 0)
    nb = tm // WINDOW

    def stage_scores(b):
        rs = slice(b * WINDOW, (b + 1) * WINDOW)
        band = slice(b * WINDOW, b * WINDOW + BAND)
        scores = []
        for h in range(N_KV_HEADS):
            c0 = h * Q_PER_KV * HEAD_DIM
            lhs = jnp.concatenate(
                [q_scr[rs, c0 + j * LANES:c0 + (j + 1) * LANES] for j in range(PAIRS)], axis=0)
            kcat = jnp.concatenate([k_scr[h, 0, band, :], k_scr[h, 1, band, :]], axis=0)
            scores.append(lax.dot_general(lhs, kcat, (((1,), (1,)), ((), ())),
                                          preferred_element_type=F32))
        return scores

    def stage_softmax(b, scores):
        bias = bias_ref[first_tile] if b == 0 else bias_ref[0]
        pmats = []
        sinks = []
        for h in range(N_KV_HEADS):
            s = scores[h]
            p_rows = []
            sink_terms = []
            for j in range(PAIRS):
                p_cols = []
                e_sink = []
                for p in range(2):
                    sub = s[j * WINDOW:(j + 1) * WINDOW, p * BAND:(p + 1) * BAND] + bias
                    sk = sink_ref[h * Q_PER_KV + 2 * j + p]
                    m = jnp.maximum(jnp.max(sub, axis=1, keepdims=True), sk)
                    p_cols.append(jnp.exp(sub - m).astype(BF16))
                    e_sink.append(jnp.exp(sk - m))
                p_rows.append(jnp.concatenate(p_cols, axis=1))
                sink_terms.append(jnp.where(even_lanes, e_sink[0], e_sink[1]))
            pmats.append(jnp.concatenate(p_rows, axis=0))
            sinks.append(sink_terms)
        return pmats, sinks

    def stage_values(b, pmats, sinks):
        rs = slice(b * WINDOW, (b + 1) * WINDOW)
        band = slice(b * WINDOW, b * WINDOW + BAND)
        for h in range(N_KV_HEADS):
            c0 = h * Q_PER_KV * HEAD_DIM
            vcat = jnp.concatenate([v_scr[h, 0, band, :], v_scr[h, 1, band, :]], axis=0)
            oa = jnp.dot(pmats[h], vcat, preferred_element_type=F32)
            for j in range(PAIRS):
                js = slice(j * WINDOW, (j + 1) * WINDOW)
                den = oa[js, LANES:] + sinks[h][j]
                o_ref[rs, c0 + j * LANES:c0 + (j + 1) * LANES] = (oa[js, :LANES] / den).astype(BF16)

    scores = stage_scores(0)
    for b in range(nb):
        nxt = stage_scores(b + 1) if b + 1 < nb else None
        pmats, sinks = stage_softmax(b, scores)
        stage_values(b, pmats, sinks)
        scores = nxt


def _attention(sinks, x, w_q, b_q, cos_q, sin_q, kmat, vmat, bias, layer):
    S = x.shape[0]
    tm = TM_ATT
    nb = tm // WINDOW
    prev = lambda i, s: (0, 0, jnp.maximum(i * nb - 1, 0), 0)
    cur = lambda i, s: (0, 0, i, 0)
    grid_spec = pltpu.PrefetchScalarGridSpec(
        num_scalar_prefetch=1,
        grid=(S // tm,),
        in_specs=[
            pl.BlockSpec((tm, D_MODEL), lambda i, s: (i, 0)),
            _resident(w_q.shape),
            _layer_slab(b_q.shape[1:], layer),
            pl.BlockSpec((tm, LANES), lambda i, s: (i, 0)),
            pl.BlockSpec((tm, LANES), lambda i, s: (i, 0)),
            pl.BlockSpec((N_KV_HEADS, 2, WINDOW, LANES), prev),
            pl.BlockSpec((N_KV_HEADS, 2, tm, LANES), cur),
            pl.BlockSpec((N_KV_HEADS, 2, WINDOW, 2 * LANES), prev),
            pl.BlockSpec((N_KV_HEADS, 2, tm, 2 * LANES), cur),
            _resident(bias.shape),
        ],
        out_specs=pl.BlockSpec((tm, D_MODEL), lambda i, s: (i, 0)),
        scratch_shapes=[
            pltpu.VMEM((tm, D_MODEL), BF16),
            pltpu.VMEM((N_KV_HEADS, 2, tm + WINDOW, LANES), BF16),
            pltpu.VMEM((N_KV_HEADS, 2, tm + WINDOW, 2 * LANES), BF16),
        ],
    )
    return pl.pallas_call(
        _attn_kernel,
        out_shape=jax.ShapeDtypeStruct((S, D_MODEL), BF16),
        grid_spec=grid_spec,
        compiler_params=_cparams("parallel"),
        name="swa_attention",
    )(sinks, x, w_q, b_q, cos_q, sin_q, kmat, kmat, vmat, vmat, bias)


def _q_perm():
    perm = np.zeros(D_MODEL, np.int32)
    for h in range(N_KV_HEADS):
        for j in range(PAIRS):
            for half in range(2):
                for p in range(2):
                    for d in range(HALF):
                        new = h * Q_PER_KV * HEAD_DIM + j * LANES + half * 64 + p * HALF + d
                        old = (h * Q_PER_KV + 2 * j + p) * HEAD_DIM + half * HALF + d
                        perm[new] = old
    return perm


def _k_perm():
    perm = np.zeros(KV_WIDTH, np.int32)
    for half in range(2):
        for h in range(N_KV_HEADS):
            for d in range(HALF):
                perm[half * LANES + h * HALF + d] = h * HEAD_DIM + half * HALF + d
    return perm


def _placements():
    ek = np.zeros((KV_WIDTH, N_KV_HEADS * 2 * LANES), np.float32)
    ev = np.zeros((KV_WIDTH, N_KV_HEADS * 2 * 2 * LANES), np.float32)
    vone = np.zeros((1, N_KV_HEADS * 2 * 2 * LANES), np.float32)
    for h in range(N_KV_HEADS):
        for p in range(2):
            for half in range(2):
                for d in range(HALF):
                    ek[half * LANES + h * HALF + d,
                       (h * 2 + p) * LANES + half * 64 + p * HALF + d] = 1.0
            for d in range(HEAD_DIM):
                base = (h * 2 + p) * 2 * LANES
                ev[h * HEAD_DIM + d, base + p * HEAD_DIM + d] = 1.0
                vone[0, base + LANES + p * HEAD_DIM + d] = 1.0
    return ek, ev, vone


def _band_bias():
    i = np.arange(WINDOW)[:, None]
    j = np.arange(BAND)[None, :]
    in_band = (j > i) & (j <= i + WINDOW)
    bias = np.zeros((2, WINDOW, BAND), np.float32)
    bias[0] = np.where(in_band, 0.0, NEG)
    bias[1] = np.where(in_band & (j >= WINDOW), 0.0, NEG)
    return bias


def kernel(x, a_w_in, a_b_in, a_ln_v_g, a_ln_v_b, a_w_s, a_b_s, a_w_out, kv_w, kv_b, b_w_q,
           b_b_q, b_sinks, b_w_o, mlp_w_up, mlp_w_down, ln_g, ln_b):
    B, S, _ = x.shape
    assert B == 1 and S % TM_MLP == 0
    h = x.reshape(S, D_MODEL)

    inv_freq = ROPE_THETA ** (-jnp.arange(0, HEAD_DIM, 2, dtype=F32) / HEAD_DIM)
    ang = jnp.arange(S, dtype=jnp.int32).astype(F32)[:, None] * inv_freq[None, :]
    cos = jnp.cos(ang)
    sin = jnp.sin(ang)
    cos_k = jnp.tile(cos, (1, 4))
    sin_k = jnp.tile(sin, (1, 4))
    scale = HEAD_DIM ** -0.5
    cos_q = cos_k * scale
    sin_q = jnp.concatenate([-sin, -sin, sin, sin], axis=1) * scale

    q_perm = _q_perm()
    k_perm = _k_perm()
    ek, ev, vone = _placements()
    ek = jnp.asarray(ek, BF16)
    ev = jnp.asarray(ev, BF16)
    vone = jnp.asarray(vone, F32)
    bias = jnp.asarray(_band_bias(), F32)

    rows = lambda v: v.reshape(-1, 1, v.shape[-1])
    b_in, ln_v_g, ln_v_b = rows(a_b_in), rows(a_ln_v_g), rows(a_ln_v_b)
    b_q = rows(b_b_q[:, q_perm])
    g_rows, b_rows = rows(ln_g), rows(ln_b)
    bs = jnp.broadcast_to(a_b_s[:, :, :, None], (N_A_LAYERS, A_GROUPS, CHUNK, A_GROUP_DIM))

    def mixer_casts(layer):
        if layer < N_A_LAYERS:
            return [(a_w_in, layer), (a_w_out, layer)]
        return [(b_w_q, layer - N_A_LAYERS), (b_w_o, layer - N_A_LAYERS)]

    def layer_casts(layer):
        return mixer_casts(layer) + [(mlp_w_up, layer), (mlp_w_down, layer)]

    w_a = a_w_in[0].astype(BF16)
    for layer in range(DEPTH):
        nxt = layer_casts(layer + 1) if layer + 1 < DEPTH else []
        if layer < N_A_LAYERS:
            first = [(a_w_out, 0), (mlp_w_up, 0), (mlp_w_down, 0)] if layer == 0 else []
            y, *done = _sgu(h, w_a, b_in, ln_v_g, ln_v_b, a_w_s, bs, layer, first)
            if layer == 0:
                w_b, w_up, w_down = done
        else:
            if layer == N_A_LAYERS:
                w_kv = jnp.concatenate([kv_w[:, :KV_WIDTH][:, k_perm], kv_w[:, KV_WIDTH:]], axis=1)
                b_kv = jnp.concatenate([kv_b[:KV_WIDTH][k_perm], kv_b[KV_WIDTH:]])
                kmat, vmat = _kv_proj(h, w_kv.astype(BF16), b_kv.reshape(1, -1), cos_k, sin_k,
                                      ek, ev, vone)
            j = layer - N_A_LAYERS
            y = _attention(b_sinks[j], h, w_a[:, q_perm], b_q, cos_q, sin_q, kmat, vmat, bias, j)
        h = _proj_res_ln(y, w_b, h, g_rows, b_rows, 2 * layer)
        h, *done = _mlp(h, w_up, w_down, g_rows, b_rows, layer, nxt)
        if nxt:
            w_a, w_b, w_up, w_down = done
    return h.reshape(B, S, D_MODEL)
```

```python
import functools

import numpy as np
import jax
import jax.numpy as jnp
from jax import lax
from jax.experimental import pallas as pl
from jax.experimental.pallas import tpu as pltpu

D_MODEL = 2048
DEPTH = 4
N_A_LAYERS = DEPTH // 2
CHUNK = 128
A_WIDTH = D_MODEL
A_GROUPS = 8
A_GROUP_DIM = A_WIDTH // A_GROUPS
HEAD_DIM = 64
HALF = HEAD_DIM // 2
N_Q_HEADS = D_MODEL // HEAD_DIM
N_KV_HEADS = 4
Q_PER_KV = N_Q_HEADS // N_KV_HEADS
PAIRS = Q_PER_KV // 2
WINDOW = 128
BAND = 2 * WINDOW
ROPE_THETA = 10000.0
D_FF = 4 * D_MODEL
LN_EPS = 1e-5
DEEPNORM_ALPHA = (2.0 * DEPTH) ** 0.25
KV_WIDTH = N_KV_HEADS * HEAD_DIM

LANES = 128
V7X_VMEM_LIMIT = 56 * 1024 * 1024

BF16 = jnp.bfloat16
F32 = jnp.float32
NEG = float(np.finfo(np.float32).min)

TM_SGU = 512
TM_PROJ = 1024
TM_MLP = 1024
TF_MLP = 512
TM_KV = 512
TM_ATT = 512
ROWS_LN = 256


def _cparams(*sem):
    return pltpu.CompilerParams(dimension_semantics=sem, vmem_limit_bytes=V7X_VMEM_LIMIT)


def _resident(shape):
    nd = len(shape)
    return pl.BlockSpec(shape, lambda *_: (0,) * nd, pipeline_mode=pl.Buffered(1))


def _layer_slab(tail, layer):
    nd = len(tail)
    return pl.BlockSpec((None,) + tuple(tail), lambda *_: (layer,) + (0,) * nd,
                        pipeline_mode=pl.Buffered(1))


BF16_SUBLANES = 16


def _cast_specs(stacked, layer, nsteps, step_of):
    _, nrows, ncols = stacked.shape
    nblk = nsteps
    while nrows % nblk or (nrows // nblk) % BF16_SUBLANES:
        nblk //= 2
    rb, group = nrows // nblk, nsteps // nblk
    in_spec = pl.BlockSpec((None, rb, ncols), lambda *ids: (layer, step_of(*ids) // group, 0))
    out_spec = pl.BlockSpec((rb, ncols), lambda *ids: (step_of(*ids) // group, 0))
    return in_spec, out_spec, jax.ShapeDtypeStruct((nrows, ncols), BF16)


def _run_casts(srcs, dsts):
    for src, dst in zip(srcs, dsts):
        dst[...] = src[...].astype(BF16)


def _gelu(t):
    return 0.5 * t * (1.0 + lax.erf(t * np.float32(np.sqrt(0.5))))


def _layer_norm(t, g, b):
    mu = jnp.mean(t, axis=-1, keepdims=True)
    c = t - mu
    var = jnp.mean(c * c, axis=-1, keepdims=True)
    return c * lax.rsqrt(var + LN_EPS) * g + b


def _sgu_kernel(n_cast, x_ref, win_ref, bin_ref, g_ref, b_ref, ws_ref, bs_ref, *rest):
    cast_src, y_ref, cast_dst = rest[:n_cast], rest[n_cast], rest[n_cast + 1:2 * n_cast + 1]
    (v_scr,) = rest[2 * n_cast + 1:]
    _run_casts(cast_src, cast_dst)
    tm = x_ref.shape[0]
    xb = x_ref[...].astype(BF16)

    def gelu_proj(cs):
        z = jnp.dot(xb, win_ref[:, cs], preferred_element_type=F32) + bin_ref[:, cs]
        return _gelu(z)

    ncol = 512
    for c in range(A_WIDTH // ncol):
        v_scr[:, c * ncol:(c + 1) * ncol] = gelu_proj(slice(A_WIDTH + c * ncol, A_WIDTH + (c + 1) * ncol))
    groups = [slice(g * A_GROUP_DIM, (g + 1) * A_GROUP_DIM) for g in range(A_GROUPS)]
    u_next = gelu_proj(groups[0])
    v_scr[...] = _layer_norm(v_scr[...], g_ref[...], b_ref[...])
    row = lax.broadcasted_iota(jnp.int32, (CHUNK, CHUNK), 0)
    col = lax.broadcasted_iota(jnp.int32, (CHUNK, CHUNK), 1)
    causal = col <= row
    for g, gs in enumerate(groups):
        u = u_next
        if g + 1 < A_GROUPS:
            u_next = gelu_proj(groups[g + 1])
        wsg = jnp.where(causal, ws_ref[g], 0.0).astype(BF16)
        bsg = bs_ref[g]
        for c in range(tm // CHUNK):
            rs = slice(c * CHUNK, (c + 1) * CHUNK)
            s = jnp.dot(wsg, v_scr[rs, gs].astype(BF16), preferred_element_type=F32) + bsg
            y_ref[rs, gs] = (u[rs, :] * s).astype(BF16)


def _sgu(x, w_in, b_in, g, b, w_s, b_s, layer, casts=()):
    S = x.shape[0]
    tm = TM_SGU
    nsteps = S // tm
    cast_specs = [_cast_specs(a, l, nsteps, lambda i: i) for a, l in casts]
    return pl.pallas_call(
        functools.partial(_sgu_kernel, len(casts)),
        out_shape=[jax.ShapeDtypeStruct((S, A_WIDTH), BF16)] + [c[2] for c in cast_specs],
        grid=(nsteps,),
        in_specs=[
            pl.BlockSpec((tm, D_MODEL), lambda i: (i, 0)),
            _resident(w_in.shape),
            _layer_slab((1, 2 * A_WIDTH), layer),
            _layer_slab((1, A_WIDTH), layer),
            _layer_slab((1, A_WIDTH), layer),
            _layer_slab((A_GROUPS, CHUNK, CHUNK), layer),
            _layer_slab((A_GROUPS, CHUNK, A_GROUP_DIM), layer),
        ] + [c[0] for c in cast_specs],
        out_specs=[pl.BlockSpec((tm, A_WIDTH), lambda i: (i, 0))] + [c[1] for c in cast_specs],
        scratch_shapes=[pltpu.VMEM((tm, A_WIDTH), F32)],
        compiler_params=_cparams("arbitrary"),
        name="sgu_mix",
    )(x, w_in, b_in, g, b, w_s, b_s, *[a for a, _ in casts])


def _proj_kernel(y_ref, w_ref, x_ref, g_ref, b_ref, o_ref):
    tm = x_ref.shape[0]
    for r in range(tm // ROWS_LN):
        rs = slice(r * ROWS_LN, (r + 1) * ROWS_LN)
        mix = jnp.dot(y_ref[rs, :], w_ref[...], preferred_element_type=F32)
        o_ref[rs, :] = _layer_norm(DEEPNORM_ALPHA * x_ref[rs, :] + mix, g_ref[...], b_ref[...])


def _proj_res_ln(y, w, x, g, b, ln_row):
    S = x.shape[0]
    tm = TM_PROJ
    return pl.pallas_call(
        _proj_kernel,
        out_shape=jax.ShapeDtypeStruct((S, D_MODEL), F32),
        grid=(S // tm,),
        in_specs=[
            pl.BlockSpec((tm, y.shape[1]), lambda i: (i, 0)),
            _resident(w.shape),
            pl.BlockSpec((tm, D_MODEL), lambda i: (i, 0)),
            _layer_slab((1, D_MODEL), ln_row),
            _layer_slab((1, D_MODEL), ln_row),
        ],
        out_specs=pl.BlockSpec((tm, D_MODEL), lambda i: (i, 0)),
        compiler_params=_cparams("parallel"),
        name="proj_res_ln",
    )(y, w, x, g, b)


def _mlp_kernel(n_cast, x_ref, wu_ref, wd_ref, g_ref, b_ref, *rest):
    cast_src, o_ref, cast_dst = rest[:n_cast], rest[n_cast], rest[n_cast + 1:2 * n_cast + 1]
    xb_scr = rest[2 * n_cast + 1]
    j = pl.program_id(1)
    last = pl.num_programs(1) - 1

    def hidden():
        h = jnp.dot(xb_scr[...], wu_ref[...], preferred_element_type=F32)
        return jnp.square(jnp.maximum(h, 0.0)).astype(BF16)

    @pl.when(j == 0)
    def _():
        _run_casts(cast_src, cast_dst)
        xb_scr[...] = x_ref[...].astype(BF16)
        o_ref[...] = DEEPNORM_ALPHA * x_ref[...] + jnp.dot(hidden(), wd_ref[...],
                                                           preferred_element_type=F32)

    @pl.when(jnp.logical_and(j > 0, j < last))
    def _():
        _run_casts(cast_src, cast_dst)
        o_ref[...] += jnp.dot(hidden(), wd_ref[...], preferred_element_type=F32)

    @pl.when(j == last)
    def _():
        _run_casts(cast_src, cast_dst)
        h = hidden()
        for r in range(x_ref.shape[0] // ROWS_LN):
            rs = slice(r * ROWS_LN, (r + 1) * ROWS_LN)
            t = o_ref[rs, :] + jnp.dot(h[rs, :], wd_ref[...], preferred_element_type=F32)
            o_ref[rs, :] = _layer_norm(t, g_ref[...], b_ref[...])


def _mlp(x, w_up, w_down, g, b, layer, casts=()):
    S = x.shape[0]
    tm, tf = TM_MLP, TF_MLP
    ni, nj = S // tm, D_FF // tf
    cast_specs = [_cast_specs(a, l, ni * nj, lambda i, j: i * nj + j) for a, l in casts]
    return pl.pallas_call(
        functools.partial(_mlp_kernel, len(casts)),
        out_shape=[jax.ShapeDtypeStruct((S, D_MODEL), F32)] + [c[2] for c in cast_specs],
        grid=(ni, nj),
        in_specs=[
            pl.BlockSpec((tm, D_MODEL), lambda i, j: (i, 0)),
            pl.BlockSpec((D_MODEL, tf), lambda i, j: (0, j)),
            pl.BlockSpec((tf, D_MODEL), lambda i, j: (j, 0)),
            _layer_slab((1, D_MODEL), 2 * layer + 1),
            _layer_slab((1, D_MODEL), 2 * layer + 1),
        ] + [c[0] for c in cast_specs],
        out_specs=[pl.BlockSpec((tm, D_MODEL), lambda i, j: (i, 0))] + [c[1] for c in cast_specs],
        scratch_shapes=[pltpu.VMEM((tm, D_MODEL), BF16)],
        compiler_params=_cparams("arbitrary", "arbitrary"),
        name="mlp_res_ln",
    )(x, w_up, w_down, g, b, *[a for a, _ in casts])


def _kv_kernel(x_ref, w_ref, b_ref, cos_ref, sin_ref, ek_ref, ev_ref, vone_ref, k_out, v_out):
    xb = x_ref[...].astype(BF16)
    kv = jnp.dot(xb, w_ref[...], preferred_element_type=F32) + b_ref[...]
    k1 = kv[:, :LANES]
    k2 = kv[:, LANES:2 * LANES]
    v = kv[:, KV_WIDTH:]
    c = cos_ref[...]
    s = sin_ref[...]
    kr = jnp.concatenate([k1 * c - k2 * s, k2 * c + k1 * s], axis=1).astype(BF16)
    kp = jnp.dot(kr, ek_ref[...], preferred_element_type=F32)
    vp = jnp.dot(v.astype(BF16), ev_ref[...], preferred_element_type=F32) + vone_ref[...]
    for h in range(N_KV_HEADS):
        for p in range(2):
            k0 = (h * 2 + p) * LANES
            k_out[h, p] = kp[:, k0:k0 + LANES].astype(BF16)
            v0 = (h * 2 + p) * 2 * LANES
            v_out[h, p] = vp[:, v0:v0 + 2 * LANES].astype(BF16)


def _kv_proj(x, w, b, cos_k, sin_k, ek, ev, vone):
    S = x.shape[0]
    tm = TM_KV
    return pl.pallas_call(
        _kv_kernel,
        out_shape=(jax.ShapeDtypeStruct((N_KV_HEADS, 2, S, LANES), BF16),
                   jax.ShapeDtypeStruct((N_KV_HEADS, 2, S, 2 * LANES), BF16)),
        grid=(S // tm,),
        in_specs=[
            pl.BlockSpec((tm, D_MODEL), lambda i: (i, 0)),
            _resident(w.shape),
            _resident(b.shape),
            pl.BlockSpec((tm, LANES), lambda i: (i, 0)),
            pl.BlockSpec((tm, LANES), lambda i: (i, 0)),
            _resident(ek.shape),
            _resident(ev.shape),
            _resident(vone.shape),
        ],
        out_specs=(pl.BlockSpec((N_KV_HEADS, 2, tm, LANES), lambda i: (0, 0, i, 0)),
                   pl.BlockSpec((N_KV_HEADS, 2, tm, 2 * LANES), lambda i: (0, 0, i, 0))),
        compiler_params=_cparams("parallel"),
        name="kv_proj",
    )(x, w, b, cos_k, sin_k, ek, ev, vone)


def _attn_kernel(sink_ref, x_ref, wq_ref, bq_ref, cos_ref, sin_ref, kp_ref, kc_ref, vp_ref,
                 vc_ref, bias_ref, o_ref, q_scr, k_scr, v_scr):
    tm = x_ref.shape[0]
    i = pl.program_id(0)
    xb = x_ref[...].astype(BF16)
    cos = cos_ref[...]
    sin = sin_ref[...]
    first_half = lax.broadcasted_iota(jnp.int32, (tm, LANES), 1) % HEAD_DIM < HALF
    ncol = 2 * LANES
    for c in range(D_MODEL // ncol):
        cs = slice(c * ncol, (c + 1) * ncol)
        q = jnp.dot(xb, wq_ref[:, cs], preferred_element_type=F32) + bq_ref[:, cs]
        for t in range(2):
            qt = q[:, t * LANES:(t + 1) * LANES]
            partner = jnp.where(first_half, pltpu.roll(qt, LANES - HALF, 1), pltpu.roll(qt, HALF, 1))
            qt = qt * cos + partner * sin
            q_scr[:, c * ncol + t * LANES:c * ncol + (t + 1) * LANES] = qt.astype(BF16)

    k_scr[:, :, :WINDOW, :] = kp_ref[...]
    k_scr[:, :, WINDOW:, :] = kc_ref[...]
    v_scr[:, :, :WINDOW, :] = vp_ref[...]
    v_scr[:, :, WINDOW:, :] = vc_ref[...]

    lane = lax.broadcasted_iota(jnp.int32, (WINDOW, LANES), 1)
    even_lanes = lane < HEAD_DIM
    first_tile = jnp.where(i == 0, 1, 0)
    nb = tm // WINDOW

    def stage_scores(b):
        rs = slice(b * WINDOW, (b + 1) * WINDOW)
        band = slice(b * WINDOW, b * WINDOW + BAND)
        scores = []
        for h in range(N_KV_HEADS):
            c0 = h * Q_PER_KV * HEAD_DIM
            lhs = jnp.concatenate(
                [q_scr[rs, c0 + j * LANES:c0 + (j + 1) * LANES] for j in range(PAIRS)], axis=0)
            kcat = jnp.concatenate([k_scr[h, 0, band, :], k_scr[h, 1, band, :]], axis=0)
            scores.append(lax.dot_general(lhs, kcat, (((1,), (1,)), ((), ())),
                                          preferred_element_type=F32))
        return scores

    def stage_softmax(b, scores):
        bias = bias_ref[first_tile] if b == 0 else bias_ref[0]
        pmats = []
        sinks = []
        for h in range(N_KV_HEADS):
            s = scores[h]
            p_rows = []
            sink_terms = []
            for j in range(PAIRS):
                p_cols = []
                e_sink = []
                for p in range(2):
                    sub = s[j * WINDOW:(j + 1) * WINDOW, p * BAND:(p + 1) * BAND] + bias
                    sk = sink_ref[h * Q_PER_KV + 2 * j + p]
                    m = jnp.maximum(jnp.max(sub, axis=1, keepdims=True), sk)
                    p_cols.append(jnp.exp(sub - m).astype(BF16))
                    e_sink.append(jnp.exp(sk - m))
                p_rows.append(jnp.concatenate(p_cols, axis=1))
                sink_terms.append(jnp.where(even_lanes, e_sink[0], e_sink[1]))
            pmats.append(jnp.concatenate(p_rows, axis=0))
            sinks.append(sink_terms)
        return pmats, sinks

    def stage_values(b, pmats, sinks):
        rs = slice(b * WINDOW, (b + 1) * WINDOW)
        band = slice(b * WINDOW, b * WINDOW + BAND)
        for h in range(N_KV_HEADS):
            c0 = h * Q_PER_KV * HEAD_DIM
            vcat = jnp.concatenate([v_scr[h, 0, band, :], v_scr[h, 1, band, :]], axis=0)
            oa = jnp.dot(pmats[h], vcat, preferred_element_type=F32)
            for j in range(PAIRS):
                js = slice(j * WINDOW, (j + 1) * WINDOW)
                den = oa[js, LANES:] + sinks[h][j]
                o_ref[rs, c0 + j * LANES:c0 + (j + 1) * LANES] = (oa[js, :LANES] / den).astype(BF16)

    scores = stage_scores(0)
    for b in range(nb):
        nxt = stage_scores(b + 1) if b + 1 < nb else None
        pmats, sinks = stage_softmax(b, scores)
        stage_values(b, pmats, sinks)
        scores = nxt


def _attention(sinks, x, w_q, b_q, cos_q, sin_q, kmat, vmat, bias, layer):
    S = x.shape[0]
    tm = TM_ATT
    nb = tm // WINDOW
    prev = lambda i, s: (0, 0, jnp.maximum(i * nb - 1, 0), 0)
    cur = lambda i, s: (0, 0, i, 0)
    grid_spec = pltpu.PrefetchScalarGridSpec(
        num_scalar_prefetch=1,
        grid=(S // tm,),
        in_specs=[
            pl.BlockSpec((tm, D_MODEL), lambda i, s: (i, 0)),
            _resident(w_q.shape),
            _layer_slab(b_q.shape[1:], layer),
            pl.BlockSpec((tm, LANES), lambda i, s: (i, 0)),
            pl.BlockSpec((tm, LANES), lambda i, s: (i, 0)),
            pl.BlockSpec((N_KV_HEADS, 2, WINDOW, LANES), prev),
            pl.BlockSpec((N_KV_HEADS, 2, tm, LANES), cur),
            pl.BlockSpec((N_KV_HEADS, 2, WINDOW, 2 * LANES), prev),
            pl.BlockSpec((N_KV_HEADS, 2, tm, 2 * LANES), cur),
            _resident(bias.shape),
        ],
        out_specs=pl.BlockSpec((tm, D_MODEL), lambda i, s: (i, 0)),
        scratch_shapes=[
            pltpu.VMEM((tm, D_MODEL), BF16),
            pltpu.VMEM((N_KV_HEADS, 2, tm + WINDOW, LANES), BF16),
            pltpu.VMEM((N_KV_HEADS, 2, tm + WINDOW, 2 * LANES), BF16),
        ],
    )
    return pl.pallas_call(
        _attn_kernel,
        out_shape=jax.ShapeDtypeStruct((S, D_MODEL), BF16),
        grid_spec=grid_spec,
        compiler_params=_cparams("parallel"),
        name="swa_attention",
    )(sinks, x, w_q, b_q, cos_q, sin_q, kmat, kmat, vmat, vmat, bias)


def _k_perm():
    perm = np.zeros(KV_WIDTH, np.int32)
    for half in range(2):
        for h in range(N_KV_HEADS):
            for d in range(HALF):
                perm[half * LANES + h * HALF + d] = h * HEAD_DIM + half * HALF + d
    return perm


def _placements():
    ek = np.zeros((KV_WIDTH, N_KV_HEADS * 2 * LANES), np.float32)
    ev = np.zeros((KV_WIDTH, N_KV_HEADS * 2 * 2 * LANES), np.float32)
    vone = np.zeros((1, N_KV_HEADS * 2 * 2 * LANES), np.float32)
    for h in range(N_KV_HEADS):
        for p in range(2):
            for half in range(2):
                for d in range(HALF):
                    ek[half * LANES + h * HALF + d,
                       (h * 2 + p) * LANES + p * HEAD_DIM + half * HALF + d] = 1.0
            for d in range(HEAD_DIM):
                base = (h * 2 + p) * 2 * LANES
                ev[h * HEAD_DIM + d, base + p * HEAD_DIM + d] = 1.0
                vone[0, base + LANES + p * HEAD_DIM + d] = 1.0
    return ek, ev, vone


def _band_bias():
    i = np.arange(WINDOW)[:, None]
    j = np.arange(BAND)[None, :]
    in_band = (j > i) & (j <= i + WINDOW)
    bias = np.zeros((2, WINDOW, BAND), np.float32)
    bias[0] = np.where(in_band, 0.0, NEG)
    bias[1] = np.where(in_band & (j >= WINDOW), 0.0, NEG)
    return bias


def kernel(x, a_w_in, a_b_in, a_ln_v_g, a_ln_v_b, a_w_s, a_b_s, a_w_out, kv_w, kv_b, b_w_q,
           b_b_q, b_sinks, b_w_o, mlp_w_up, mlp_w_down, ln_g, ln_b):
    B, S, _ = x.shape
    assert B == 1 and S % TM_MLP == 0
    h = x.reshape(S, D_MODEL)

    inv_freq = ROPE_THETA ** (-jnp.arange(0, HEAD_DIM, 2, dtype=F32) / HEAD_DIM)
    ang = jnp.arange(S, dtype=jnp.int32).astype(F32)[:, None] * jnp.tile(inv_freq, 4)[None, :]
    cos_k = jnp.cos(ang)
    sin_k = jnp.sin(ang)
    scale = HEAD_DIM ** -0.5
    rot_sign = np.tile(np.repeat(np.array([-scale, scale], np.float32), HALF), 2)
    cos_q = cos_k * scale
    sin_q = sin_k * rot_sign[None, :]

    k_perm = _k_perm()
    ek, ev, vone = _placements()
    ek = jnp.asarray(ek, BF16)
    ev = jnp.asarray(ev, BF16)
    vone = jnp.asarray(vone, F32)
    bias = jnp.asarray(_band_bias(), F32)

    rows = lambda v: v.reshape(-1, 1, v.shape[-1])
    b_in, ln_v_g, ln_v_b = rows(a_b_in), rows(a_ln_v_g), rows(a_ln_v_b)
    b_q = rows(b_b_q)
    g_rows, b_rows = rows(ln_g), rows(ln_b)
    bs = jnp.broadcast_to(a_b_s[:, :, :, None], (N_A_LAYERS, A_GROUPS, CHUNK, A_GROUP_DIM))

    def mixer_casts(layer):
        if layer < N_A_LAYERS:
            return [(a_w_in, layer), (a_w_out, layer)]
        return [(b_w_q, layer - N_A_LAYERS), (b_w_o, layer - N_A_LAYERS)]

    def layer_casts(layer):
        return mixer_casts(layer) + [(mlp_w_up, layer), (mlp_w_down, layer)]

    w_a = a_w_in[0].astype(BF16)
    for layer in range(DEPTH):
        nxt = layer_casts(layer + 1) if layer + 1 < DEPTH else []
        if layer < N_A_LAYERS:
            first = [(a_w_out, 0), (mlp_w_up, 0), (mlp_w_down, 0)] if layer == 0 else []
            y, *done = _sgu(h, w_a, b_in, ln_v_g, ln_v_b, a_w_s, bs, layer, first)
            if layer == 0:
                w_b, w_up, w_down = done
        else:
            if layer == N_A_LAYERS:
                w_kv = jnp.concatenate([kv_w[:, :KV_WIDTH][:, k_perm], kv_w[:, KV_WIDTH:]], axis=1)
                b_kv = jnp.concatenate([kv_b[:KV_WIDTH][k_perm], kv_b[KV_WIDTH:]])
                kmat, vmat = _kv_proj(h, w_kv.astype(BF16), b_kv.reshape(1, -1), cos_k, sin_k,
                                      ek, ev, vone)
            j = layer - N_A_LAYERS
            y = _attention(b_sinks[j], h, w_a, b_q, cos_q, sin_q, kmat, vmat, bias, j)
        h = _proj_res_ln(y, w_b, h, g_rows, b_rows, 2 * layer)
        h, *done = _mlp(h, w_up, w_down, g_rows, b_rows, layer, nxt)
        if nxt:
            w_a, w_b, w_up, w_down = done
    return h.reshape(B, S, D_MODEL)
```

```python
import functools

import numpy as np
import jax
import jax.numpy as jnp
from jax import lax
from jax.experimental import pallas as pl
from jax.experimental.pallas import tpu as pltpu

D_MODEL = 2048
DEPTH = 4
N_A_LAYERS = DEPTH // 2
CHUNK = 128
A_WIDTH = D_MODEL
A_GROUPS = 8
A_GROUP_DIM = A_WIDTH // A_GROUPS
HEAD_DIM = 64
HALF = HEAD_DIM // 2
N_Q_HEADS = D_MODEL // HEAD_DIM
N_KV_HEADS = 4
Q_PER_KV = N_Q_HEADS // N_KV_HEADS
PAIRS = Q_PER_KV // 2
WINDOW = 128
BAND = 2 * WINDOW
ROPE_THETA = 10000.0
D_FF = 4 * D_MODEL
LN_EPS = 1e-5
DEEPNORM_ALPHA = (2.0 * DEPTH) ** 0.25
KV_WIDTH = N_KV_HEADS * HEAD_DIM

LANES = 128
V7X_VMEM_LIMIT = 56 * 1024 * 1024

BF16 = jnp.bfloat16
F32 = jnp.float32
NEG_BF16 = float(jnp.finfo(jnp.bfloat16).min)
LOG2_E = float(np.log2(np.e))

TM_SGU = 512
TM_PROJ = 1024
TM_MLP = 1024
TF_MLP = 512
TM_KV = 512
TM_ATT = 512
ROWS_LN = 256


def _cparams(*sem):
    return pltpu.CompilerParams(dimension_semantics=sem, vmem_limit_bytes=V7X_VMEM_LIMIT)


def _resident(shape):
    nd = len(shape)
    return pl.BlockSpec(shape, lambda *_: (0,) * nd, pipeline_mode=pl.Buffered(1))


def _layer_slab(tail, layer):
    nd = len(tail)
    return pl.BlockSpec((None,) + tuple(tail), lambda *_: (layer,) + (0,) * nd,
                        pipeline_mode=pl.Buffered(1))


BF16_SUBLANES = 16


def _cast_specs(stacked, layer, nsteps, step_of):
    _, nrows, ncols = stacked.shape
    nblk = nsteps
    while nrows % nblk or (nrows // nblk) % BF16_SUBLANES:
        nblk //= 2
    rb, group = nrows // nblk, nsteps // nblk
    in_spec = pl.BlockSpec((None, rb, ncols), lambda *ids: (layer, step_of(*ids) // group, 0))
    out_spec = pl.BlockSpec((rb, ncols), lambda *ids: (step_of(*ids) // group, 0))
    return in_spec, out_spec, jax.ShapeDtypeStruct((nrows, ncols), BF16)


def _run_casts(srcs, dsts):
    for src, dst in zip(srcs, dsts):
        dst[...] = src[...].astype(BF16)


def _gelu(t):
    return 0.5 * t * (1.0 + lax.erf(t * np.float32(np.sqrt(0.5))))


def _layer_norm(t, g, b):
    mu = jnp.mean(t, axis=-1, keepdims=True)
    c = t - mu
    var = jnp.mean(c * c, axis=-1, keepdims=True)
    return c * lax.rsqrt(var + LN_EPS) * g + b


def _sgu_kernel(n_cast, x_ref, win_ref, bin_ref, g_ref, b_ref, ws_ref, bs_ref, *rest):
    cast_src, y_ref, cast_dst = rest[:n_cast], rest[n_cast], rest[n_cast + 1:2 * n_cast + 1]
    (v_scr,) = rest[2 * n_cast + 1:]
    _run_casts(cast_src, cast_dst)
    tm = x_ref.shape[0]
    xb = x_ref[...].astype(BF16)

    def gelu_proj(cs):
        z = jnp.dot(xb, win_ref[:, cs], preferred_element_type=F32) + bin_ref[:, cs]
        return _gelu(z)

    ncol = 512
    for c in range(A_WIDTH // ncol):
        v_scr[:, c * ncol:(c + 1) * ncol] = gelu_proj(slice(A_WIDTH + c * ncol, A_WIDTH + (c + 1) * ncol))
    groups = [slice(g * A_GROUP_DIM, (g + 1) * A_GROUP_DIM) for g in range(A_GROUPS)]
    u_next = gelu_proj(groups[0])
    v_scr[...] = _layer_norm(v_scr[...], g_ref[...], b_ref[...])
    row = lax.broadcasted_iota(jnp.int32, (CHUNK, CHUNK), 0)
    col = lax.broadcasted_iota(jnp.int32, (CHUNK, CHUNK), 1)
    causal = col <= row
    for g, gs in enumerate(groups):
        u = u_next
        if g + 1 < A_GROUPS:
            u_next = gelu_proj(groups[g + 1])
        wsg = jnp.where(causal, ws_ref[g], 0.0).astype(BF16)
        bsg = bs_ref[g]
        for c in range(tm // CHUNK):
            rs = slice(c * CHUNK, (c + 1) * CHUNK)
            s = jnp.dot(wsg, v_scr[rs, gs].astype(BF16), preferred_element_type=F32) + bsg
            y_ref[rs, gs] = (u[rs, :] * s).astype(BF16)


def _sgu(x, w_in, b_in, g, b, w_s, b_s, layer, casts=()):
    S = x.shape[0]
    tm = TM_SGU
    nsteps = S // tm
    cast_specs = [_cast_specs(a, l, nsteps, lambda i: i) for a, l in casts]
    return pl.pallas_call(
        functools.partial(_sgu_kernel, len(casts)),
        out_shape=[jax.ShapeDtypeStruct((S, A_WIDTH), BF16)] + [c[2] for c in cast_specs],
        grid=(nsteps,),
        in_specs=[
            pl.BlockSpec((tm, D_MODEL), lambda i: (i, 0)),
            _resident(w_in.shape),
            _layer_slab((1, 2 * A_WIDTH), layer),
            _layer_slab((1, A_WIDTH), layer),
            _layer_slab((1, A_WIDTH), layer),
            _layer_slab((A_GROUPS, CHUNK, CHUNK), layer),
            _layer_slab((A_GROUPS, CHUNK, A_GROUP_DIM), layer),
        ] + [c[0] for c in cast_specs],
        out_specs=[pl.BlockSpec((tm, A_WIDTH), lambda i: (i, 0))] + [c[1] for c in cast_specs],
        scratch_shapes=[pltpu.VMEM((tm, A_WIDTH), F32)],
        compiler_params=_cparams("arbitrary"),
        name="sgu_mix",
    )(x, w_in, b_in, g, b, w_s, b_s, *[a for a, _ in casts])


def _proj_kernel(y_ref, w_ref, x_ref, g_ref, b_ref, o_ref):
    tm = x_ref.shape[0]
    for r in range(tm // ROWS_LN):
        rs = slice(r * ROWS_LN, (r + 1) * ROWS_LN)
        mix = jnp.dot(y_ref[rs, :], w_ref[...], preferred_element_type=F32)
        o_ref[rs, :] = _layer_norm(DEEPNORM_ALPHA * x_ref[rs, :] + mix, g_ref[...], b_ref[...])


def _proj_res_ln(y, w, x, g, b, ln_row):
    S = x.shape[0]
    tm = TM_PROJ
    return pl.pallas_call(
        _proj_kernel,
        out_shape=jax.ShapeDtypeStruct((S, D_MODEL), F32),
        grid=(S // tm,),
        in_specs=[
            pl.BlockSpec((tm, y.shape[1]), lambda i: (i, 0)),
            _resident(w.shape),
            pl.BlockSpec((tm, D_MODEL), lambda i: (i, 0)),
            _layer_slab((1, D_MODEL), ln_row),
            _layer_slab((1, D_MODEL), ln_row),
        ],
        out_specs=pl.BlockSpec((tm, D_MODEL), lambda i: (i, 0)),
        compiler_params=_cparams("parallel"),
        name="proj_res_ln",
    )(y, w, x, g, b)


def _mlp_kernel(n_cast, x_ref, wu_ref, wd_ref, g_ref, b_ref, *rest):
    cast_src, o_ref, cast_dst = rest[:n_cast], rest[n_cast], rest[n_cast + 1:2 * n_cast + 1]
    xb_scr = rest[2 * n_cast + 1]
    j = pl.program_id(1)
    last = pl.num_programs(1) - 1

    def hidden():
        h = jnp.dot(xb_scr[...], wu_ref[...], preferred_element_type=F32)
        return jnp.square(jnp.maximum(h, 0.0)).astype(BF16)

    @pl.when(j == 0)
    def _():
        _run_casts(cast_src, cast_dst)
        xb_scr[...] = x_ref[...].astype(BF16)
        o_ref[...] = DEEPNORM_ALPHA * x_ref[...] + jnp.dot(hidden(), wd_ref[...],
                                                           preferred_element_type=F32)

    @pl.when(jnp.logical_and(j > 0, j < last))
    def _():
        _run_casts(cast_src, cast_dst)
        o_ref[...] += jnp.dot(hidden(), wd_ref[...], preferred_element_type=F32)

    @pl.when(j == last)
    def _():
        _run_casts(cast_src, cast_dst)
        h = hidden()
        chunks = [slice(r * ROWS_LN, (r + 1) * ROWS_LN) for r in range(x_ref.shape[0] // ROWS_LN)]
        for rs in chunks:
            o_ref[rs, :] += jnp.dot(h[rs, :], wd_ref[...], preferred_element_type=F32)
        for rs in chunks:
            o_ref[rs, :] = _layer_norm(o_ref[rs, :], g_ref[...], b_ref[...])


def _mlp(x, w_up, w_down, g, b, layer, casts=()):
    S = x.shape[0]
    tm, tf = TM_MLP, TF_MLP
    ni, nj = S // tm, D_FF // tf
    cast_specs = [_cast_specs(a, l, ni * nj, lambda i, j: i * nj + j) for a, l in casts]
    return pl.pallas_call(
        functools.partial(_mlp_kernel, len(casts)),
        out_shape=[jax.ShapeDtypeStruct((S, D_MODEL), F32)] + [c[2] for c in cast_specs],
        grid=(ni, nj),
        in_specs=[
            pl.BlockSpec((tm, D_MODEL), lambda i, j: (i, 0)),
            pl.BlockSpec((D_MODEL, tf), lambda i, j: (0, j)),
            pl.BlockSpec((tf, D_MODEL), lambda i, j: (j, 0)),
            _layer_slab((1, D_MODEL), 2 * layer + 1),
            _layer_slab((1, D_MODEL), 2 * layer + 1),
        ] + [c[0] for c in cast_specs],
        out_specs=[pl.BlockSpec((tm, D_MODEL), lambda i, j: (i, 0))] + [c[1] for c in cast_specs],
        scratch_shapes=[pltpu.VMEM((tm, D_MODEL), BF16)],
        compiler_params=_cparams("arbitrary", "arbitrary"),
        name="mlp_res_ln",
    )(x, w_up, w_down, g, b, *[a for a, _ in casts])


def _kv_kernel(x_ref, w_ref, b_ref, cos_ref, sin_ref, ek_ref, ev_ref, vone_ref, k_out, v_out):
    xb = x_ref[...].astype(BF16)
    kv = jnp.dot(xb, w_ref[...], preferred_element_type=F32) + b_ref[...]
    k1 = kv[:, :LANES]
    k2 = kv[:, LANES:2 * LANES]
    v = kv[:, KV_WIDTH:]
    c = cos_ref[...]
    s = sin_ref[...]
    kr = jnp.concatenate([k1 * c - k2 * s, k2 * c + k1 * s], axis=1).astype(BF16)
    kp = jnp.dot(kr, ek_ref[...], preferred_element_type=F32)
    vp = jnp.dot(v.astype(BF16), ev_ref[...], preferred_element_type=F32) + vone_ref[...]
    for h in range(N_KV_HEADS):
        for p in range(2):
            k0 = (h * 2 + p) * LANES
            k_out[h, p] = kp[:, k0:k0 + LANES].astype(BF16)
            v0 = (h * 2 + p) * 2 * LANES
            v_out[h, p] = vp[:, v0:v0 + 2 * LANES].astype(BF16)


def _kv_proj(x, w, b, cos_k, sin_k, ek, ev, vone):
    S = x.shape[0]
    tm = TM_KV
    return pl.pallas_call(
        _kv_kernel,
        out_shape=(jax.ShapeDtypeStruct((N_KV_HEADS, 2, S, LANES), BF16),
                   jax.ShapeDtypeStruct((N_KV_HEADS, 2, S, 2 * LANES), BF16)),
        grid=(S // tm,),
        in_specs=[
            pl.BlockSpec((tm, D_MODEL), lambda i: (i, 0)),
            _resident(w.shape),
            _resident(b.shape),
            pl.BlockSpec((tm, LANES), lambda i: (i, 0)),
            pl.BlockSpec((tm, LANES), lambda i: (i, 0)),
            _resident(ek.shape),
            _resident(ev.shape),
            _resident(vone.shape),
        ],
        out_specs=(pl.BlockSpec((N_KV_HEADS, 2, tm, LANES), lambda i: (0, 0, i, 0)),
                   pl.BlockSpec((N_KV_HEADS, 2, tm, 2 * LANES), lambda i: (0, 0, i, 0))),
        compiler_params=_cparams("parallel"),
        name="kv_proj",
    )(x, w, b, cos_k, sin_k, ek, ev, vone)


def _attn_kernel(sink_ref, x_ref, wq_ref, bq_ref, cos_ref, sin_ref, kp_ref, kc_ref, vp_ref,
                 vc_ref, bias_ref, eye_ref, o_ref, q_scr, k_scr, v_scr):
    tm = x_ref.shape[0]
    i = pl.program_id(0)
    xb = x_ref[...].astype(BF16)
    cos = cos_ref[...]
    sin = sin_ref[...]
    first_half = lax.broadcasted_iota(jnp.int32, (tm, LANES), 1) % HEAD_DIM < HALF
    ncol = 2 * LANES
    for c in range(D_MODEL // ncol):
        cs = slice(c * ncol, (c + 1) * ncol)
        q = jnp.dot(xb, wq_ref[:, cs], preferred_element_type=F32) + bq_ref[:, cs]
        for t in range(2):
            qt = q[:, t * LANES:(t + 1) * LANES]
            partner = jnp.where(first_half, pltpu.roll(qt, LANES - HALF, 1), pltpu.roll(qt, HALF, 1))
            qt = qt * cos + partner * sin
            q_scr[:, c * ncol + t * LANES:c * ncol + (t + 1) * LANES] = qt.astype(BF16)

    k_scr[:, :, :WINDOW, :] = kp_ref[...]
    k_scr[:, :, WINDOW:, :] = kc_ref[...]
    v_scr[:, :, :WINDOW, :] = vp_ref[...]
    v_scr[:, :, WINDOW:, :] = vc_ref[...]

    lane = lax.broadcasted_iota(jnp.int32, (WINDOW, LANES), 1)
    even_lanes = lane < HEAD_DIM
    first_tile = jnp.where(i == 0, 1, 0)
    nb = tm // WINDOW

    def stage_scores(b):
        rs = slice(b * WINDOW, (b + 1) * WINDOW)
        band = slice(b * WINDOW, b * WINDOW + BAND)
        mask_cols = bias_ref[first_tile] if b == 0 else bias_ref[0]
        scores = []
        for h in range(N_KV_HEADS):
            c0 = h * Q_PER_KV * HEAD_DIM
            lhs = jnp.concatenate(
                [q_scr[rs, c0 + j * LANES:c0 + (j + 1) * LANES] for j in range(PAIRS)], axis=0)
            kcat = jnp.concatenate([k_scr[h, 0, band, :], k_scr[h, 1, band, :]], axis=0)
            lhs = jnp.concatenate([lhs, eye_ref[...]], axis=1)
            kcat = jnp.concatenate([kcat, mask_cols], axis=1)
            scores.append(lax.dot_general(lhs, kcat, (((1,), (1,)), ((), ())),
                                          preferred_element_type=F32))
        return scores

    def stage_softmax(b, scores):
        pmats = []
        sinks = []
        for h in range(N_KV_HEADS):
            s = scores[h]
            p_rows = []
            sink_terms = []
            for j in range(PAIRS):
                p_cols = []
                sink_arg = []
                for p in range(2):
                    sub = s[j * WINDOW:(j + 1) * WINDOW, p * BAND:(p + 1) * BAND]
                    sk = sink_ref[h * Q_PER_KV + 2 * j + p]
                    m = jnp.maximum(jnp.max(sub, axis=1, keepdims=True), sk)
                    p_cols.append(jnp.exp2(sub - m).astype(BF16))
                    sink_arg.append(sk - m)
                p_rows.append(jnp.concatenate(p_cols, axis=1))
                sink_terms.append(jnp.exp2(jnp.where(even_lanes, sink_arg[0], sink_arg[1])))
            pmats.append(jnp.concatenate(p_rows, axis=0))
            sinks.append(sink_terms)
        return pmats, sinks

    def stage_values(b, pmats, sinks):
        rs = slice(b * WINDOW, (b + 1) * WINDOW)
        band = slice(b * WINDOW, b * WINDOW + BAND)
        for h in range(N_KV_HEADS):
            c0 = h * Q_PER_KV * HEAD_DIM
            vcat = jnp.concatenate([v_scr[h, 0, band, :], v_scr[h, 1, band, :]], axis=0)
            oa = jnp.dot(pmats[h], vcat, preferred_element_type=F32)
            for j in range(PAIRS):
                js = slice(j * WINDOW, (j + 1) * WINDOW)
                den = oa[js, LANES:] + sinks[h][j]
                o_ref[rs, c0 + j * LANES:c0 + (j + 1) * LANES] = (oa[js, :LANES] / den).astype(BF16)

    scores = stage_scores(0)
    for b in range(nb):
        nxt = stage_scores(b + 1) if b + 1 < nb else None
        pmats, sinks = stage_softmax(b, scores)
        stage_values(b, pmats, sinks)
        scores = nxt


def _attention(sinks, x, w_q, b_q, cos_q, sin_q, kmat, vmat, bias, eye, layer):
    S = x.shape[0]
    tm = TM_ATT
    nb = tm // WINDOW
    prev = lambda i, s: (0, 0, jnp.maximum(i * nb - 1, 0), 0)
    cur = lambda i, s: (0, 0, i, 0)
    grid_spec = pltpu.PrefetchScalarGridSpec(
        num_scalar_prefetch=1,
        grid=(S // tm,),
        in_specs=[
            pl.BlockSpec((tm, D_MODEL), lambda i, s: (i, 0)),
            _resident(w_q.shape),
            _layer_slab(b_q.shape[1:], layer),
            pl.BlockSpec((tm, LANES), lambda i, s: (i, 0)),
            pl.BlockSpec((tm, LANES), lambda i, s: (i, 0)),
            pl.BlockSpec((N_KV_HEADS, 2, WINDOW, LANES), prev),
            pl.BlockSpec((N_KV_HEADS, 2, tm, LANES), cur),
            pl.BlockSpec((N_KV_HEADS, 2, WINDOW, 2 * LANES), prev),
            pl.BlockSpec((N_KV_HEADS, 2, tm, 2 * LANES), cur),
            _resident(bias.shape),
            _resident(eye.shape),
        ],
        out_specs=pl.BlockSpec((tm, D_MODEL), lambda i, s: (i, 0)),
        scratch_shapes=[
            pltpu.VMEM((tm, D_MODEL), BF16),
            pltpu.VMEM((N_KV_HEADS, 2, tm + WINDOW, LANES), BF16),
            pltpu.VMEM((N_KV_HEADS, 2, tm + WINDOW, 2 * LANES), BF16),
        ],
    )
    return pl.pallas_call(
        _attn_kernel,
        out_shape=jax.ShapeDtypeStruct((S, D_MODEL), BF16),
        grid_spec=grid_spec,
        compiler_params=_cparams("parallel"),
        name="swa_attention",
    )(sinks, x, w_q, b_q, cos_q, sin_q, kmat, kmat, vmat, vmat, bias, eye)


def _k_perm():
    perm = np.zeros(KV_WIDTH, np.int32)
    for half in range(2):
        for h in range(N_KV_HEADS):
            for d in range(HALF):
                perm[half * LANES + h * HALF + d] = h * HEAD_DIM + half * HALF + d
    return perm


def _placements():
    ek = np.zeros((KV_WIDTH, N_KV_HEADS * 2 * LANES), np.float32)
    ev = np.zeros((KV_WIDTH, N_KV_HEADS * 2 * 2 * LANES), np.float32)
    vone = np.zeros((1, N_KV_HEADS * 2 * 2 * LANES), np.float32)
    for h in range(N_KV_HEADS):
        for p in range(2):
            for half in range(2):
                for d in range(HALF):
                    ek[half * LANES + h * HALF + d,
                       (h * 2 + p) * LANES + p * HEAD_DIM + half * HALF + d] = 1.0
            for d in range(HEAD_DIM):
                base = (h * 2 + p) * 2 * LANES
                ev[h * HEAD_DIM + d, base + p * HEAD_DIM + d] = 1.0
                vone[0, base + LANES + p * HEAD_DIM + d] = 1.0
    return ek, ev, vone


def _band_mask():
    i = np.arange(WINDOW)[None, :]
    j = np.arange(BAND)[:, None]
    in_band = (j > i) & (j <= i + WINDOW)
    mask = np.zeros((2, BAND, WINDOW), np.float32)
    mask[0] = np.where(in_band, 0.0, NEG_BF16)
    mask[1] = np.where(in_band & (j >= WINDOW), 0.0, NEG_BF16)
    eye = np.tile(np.eye(WINDOW, dtype=np.float32), (PAIRS, 1))
    return np.tile(mask, (1, 2, 1)), eye


def kernel(x, a_w_in, a_b_in, a_ln_v_g, a_ln_v_b, a_w_s, a_b_s, a_w_out, kv_w, kv_b, b_w_q,
           b_b_q, b_sinks, b_w_o, mlp_w_up, mlp_w_down, ln_g, ln_b):
    B, S, _ = x.shape
    assert B == 1 and S % TM_MLP == 0
    h = x.reshape(S, D_MODEL)

    inv_freq = ROPE_THETA ** (-jnp.arange(0, HEAD_DIM, 2, dtype=F32) / HEAD_DIM)
    ang = jnp.arange(S, dtype=jnp.int32).astype(F32)[:, None] * jnp.tile(inv_freq, 4)[None, :]
    cos_k = jnp.cos(ang)
    sin_k = jnp.sin(ang)
    scale = HEAD_DIM ** -0.5 * LOG2_E
    rot_sign = np.tile(np.repeat(np.array([-scale, scale], np.float32), HALF), 2)
    cos_q = cos_k * scale
    sin_q = sin_k * rot_sign[None, :]

    k_perm = _k_perm()
    ek, ev, vone = _placements()
    ek = jnp.asarray(ek, BF16)
    ev = jnp.asarray(ev, BF16)
    vone = jnp.asarray(vone, F32)
    mask, eye = _band_mask()
    mask = jnp.asarray(mask, BF16)
    eye = jnp.asarray(eye, BF16)

    rows = lambda v: v.reshape(-1, 1, v.shape[-1])
    b_in, ln_v_g, ln_v_b = rows(a_b_in), rows(a_ln_v_g), rows(a_ln_v_b)
    b_q = rows(b_b_q)
    g_rows, b_rows = rows(ln_g), rows(ln_b)
    bs = jnp.broadcast_to(a_b_s[:, :, :, None], (N_A_LAYERS, A_GROUPS, CHUNK, A_GROUP_DIM))

    def mixer_casts(layer):
        if layer < N_A_LAYERS:
            return [(a_w_in, layer), (a_w_out, layer)]
        return [(b_w_q, layer - N_A_LAYERS), (b_w_o, layer - N_A_LAYERS)]

    def layer_casts(layer):
        return mixer_casts(layer) + [(mlp_w_up, layer), (mlp_w_down, layer)]

    w_a = a_w_in[0].astype(BF16)
    for layer in range(DEPTH):
        nxt = layer_casts(layer + 1) if layer + 1 < DEPTH else []
        if layer < N_A_LAYERS:
            first = [(a_w_out, 0), (mlp_w_up, 0), (mlp_w_down, 0)] if layer == 0 else []
            y, *done = _sgu(h, w_a, b_in, ln_v_g, ln_v_b, a_w_s, bs, layer, first)
            if layer == 0:
                w_b, w_up, w_down = done
        else:
            if layer == N_A_LAYERS:
                w_kv = jnp.concatenate([kv_w[:, :KV_WIDTH][:, k_perm], kv_w[:, KV_WIDTH:]], axis=1)
                b_kv = jnp.concatenate([kv_b[:KV_WIDTH][k_perm], kv_b[KV_WIDTH:]])
                kmat, vmat = _kv_proj(h, w_kv.astype(BF16), b_kv.reshape(1, -1), cos_k, sin_k,
                                      ek, ev, vone)
            j = layer - N_A_LAYERS
            y = _attention(b_sinks[j] * LOG2_E, h, w_a, b_q, cos_q, sin_q, kmat, vmat, mask, eye, j)
        h = _proj_res_ln(y, w_b, h, g_rows, b_rows, 2 * layer)
        h, *done = _mlp(h, w_up, w_down, g_rows, b_rows, layer, nxt)
        if nxt:
            w_a, w_b, w_up, w_down = done
    return h.reshape(B, S, D_MODEL)
```

```python
import functools

import numpy as np
import jax
import jax.numpy as jnp
from jax import lax
from jax.experimental import pallas as pl
from jax.experimental.pallas import tpu as pltpu

D_MODEL = 2048
DEPTH = 4
N_A_LAYERS = DEPTH // 2
CHUNK = 128
A_WIDTH = D_MODEL
A_GROUPS = 8
A_GROUP_DIM = A_WIDTH // A_GROUPS
HEAD_DIM = 64
HALF = HEAD_DIM // 2
N_Q_HEADS = D_MODEL // HEAD_DIM
N_KV_HEADS = 4
Q_PER_KV = N_Q_HEADS // N_KV_HEADS
PAIRS = Q_PER_KV // 2
WINDOW = 128
BAND = 2 * WINDOW
ROPE_THETA = 10000.0
D_FF = 4 * D_MODEL
LN_EPS = 1e-5
DEEPNORM_ALPHA = (2.0 * DEPTH) ** 0.25
KV_WIDTH = N_KV_HEADS * HEAD_DIM

LANES = 128
V7X_VMEM_LIMIT = 56 * 1024 * 1024

BF16 = jnp.bfloat16
F32 = jnp.float32
NEG_BF16 = float(jnp.finfo(jnp.bfloat16).min)
LOG2_E = float(np.log2(np.e))

TM_SGU = 512
TM_PROJ = 1024
TM_MLP = 1024
TF_MLP = 512
TM_KV = 1024
TM_ATT = 512
ROWS_LN = 256
ROWS_PROJ = 256


def _cparams(*sem):
    return pltpu.CompilerParams(dimension_semantics=sem, vmem_limit_bytes=V7X_VMEM_LIMIT)


def _resident(shape):
    nd = len(shape)
    return pl.BlockSpec(shape, lambda *_: (0,) * nd, pipeline_mode=pl.Buffered(1))


def _layer_slab(tail, layer):
    nd = len(tail)
    return pl.BlockSpec((None,) + tuple(tail), lambda *_: (layer,) + (0,) * nd,
                        pipeline_mode=pl.Buffered(1))


BF16_SUBLANES = 16


def _cast_specs(stacked, layer, nsteps, step_of):
    _, nrows, ncols = stacked.shape
    nblk = nsteps
    while nrows % nblk or (nrows // nblk) % BF16_SUBLANES:
        nblk //= 2
    rb, group = nrows // nblk, nsteps // nblk
    in_spec = pl.BlockSpec((None, rb, ncols), lambda *ids: (layer, step_of(*ids) // group, 0))
    out_spec = pl.BlockSpec((rb, ncols), lambda *ids: (step_of(*ids) // group, 0))
    return in_spec, out_spec, jax.ShapeDtypeStruct((nrows, ncols), BF16)


def _run_casts(srcs, dsts):
    for src, dst in zip(srcs, dsts):
        dst[...] = src[...].astype(BF16)


def _gelu(t):
    return 0.5 * t * (1.0 + lax.erf(t * np.float32(np.sqrt(0.5))))


def _layer_norm(t, g, b):
    mu = jnp.mean(t, axis=-1, keepdims=True)
    c = t - mu
    var = jnp.mean(c * c, axis=-1, keepdims=True)
    return c * lax.rsqrt(var + LN_EPS) * g + b


def _sgu_kernel(n_cast, x_ref, win_ref, bin_ref, g_ref, b_ref, ws_ref, bs_ref, *rest):
    cast_src, y_ref, cast_dst = rest[:n_cast], rest[n_cast], rest[n_cast + 1:2 * n_cast + 1]
    (v_scr,) = rest[2 * n_cast + 1:]
    _run_casts(cast_src, cast_dst)
    tm = x_ref.shape[0]
    xb = x_ref[...].astype(BF16)

    def gelu_proj(cs):
        z = jnp.dot(xb, win_ref[:, cs], preferred_element_type=F32) + bin_ref[:, cs]
        return _gelu(z)

    ncol = 512
    for c in range(A_WIDTH // ncol):
        v_scr[:, c * ncol:(c + 1) * ncol] = gelu_proj(slice(A_WIDTH + c * ncol, A_WIDTH + (c + 1) * ncol))
    groups = [slice(g * A_GROUP_DIM, (g + 1) * A_GROUP_DIM) for g in range(A_GROUPS)]
    u_next = gelu_proj(groups[0])
    v_scr[...] = _layer_norm(v_scr[...], g_ref[...], b_ref[...])
    row = lax.broadcasted_iota(jnp.int32, (CHUNK, CHUNK), 0)
    col = lax.broadcasted_iota(jnp.int32, (CHUNK, CHUNK), 1)
    causal = col <= row
    for g, gs in enumerate(groups):
        u = u_next
        if g + 1 < A_GROUPS:
            u_next = gelu_proj(groups[g + 1])
        wsg = jnp.where(causal, ws_ref[g], 0.0).astype(BF16)
        bsg = bs_ref[g]
        for c in range(tm // CHUNK):
            rs = slice(c * CHUNK, (c + 1) * CHUNK)
            s = jnp.dot(wsg, v_scr[rs, gs].astype(BF16), preferred_element_type=F32) + bsg
            y_ref[rs, gs] = (u[rs, :] * s).astype(BF16)


def _sgu(x, w_in, b_in, g, b, w_s, b_s, layer, casts=()):
    S = x.shape[0]
    tm = TM_SGU
    nsteps = S // tm
    cast_specs = [_cast_specs(a, l, nsteps, lambda i: i) for a, l in casts]
    return pl.pallas_call(
        functools.partial(_sgu_kernel, len(casts)),
        out_shape=[jax.ShapeDtypeStruct((S, A_WIDTH), BF16)] + [c[2] for c in cast_specs],
        grid=(nsteps,),
        in_specs=[
            pl.BlockSpec((tm, D_MODEL), lambda i: (i, 0)),
            _resident(w_in.shape),
            _layer_slab((1, 2 * A_WIDTH), layer),
            _layer_slab((1, A_WIDTH), layer),
            _layer_slab((1, A_WIDTH), layer),
            _layer_slab((A_GROUPS, CHUNK, CHUNK), layer),
            _layer_slab((A_GROUPS, CHUNK, A_GROUP_DIM), layer),
        ] + [c[0] for c in cast_specs],
        out_specs=[pl.BlockSpec((tm, A_WIDTH), lambda i: (i, 0))] + [c[1] for c in cast_specs],
        scratch_shapes=[pltpu.VMEM((tm, A_WIDTH), F32)],
        compiler_params=_cparams("arbitrary"),
        name="sgu_mix",
    )(x, w_in, b_in, g, b, w_s, b_s, *[a for a, _ in casts])


def _proj_kernel(y_ref, w_ref, x_ref, g_ref, b_ref, o_ref):
    tm = x_ref.shape[0]
    for r in range(tm // ROWS_PROJ):
        rs = slice(r * ROWS_PROJ, (r + 1) * ROWS_PROJ)
        mix = jnp.dot(y_ref[rs, :], w_ref[...], preferred_element_type=F32)
        o_ref[rs, :] = _layer_norm(DEEPNORM_ALPHA * x_ref[rs, :] + mix, g_ref[...], b_ref[...])


def _proj_res_ln(y, w, x, g, b, ln_row):
    S = x.shape[0]
    tm = TM_PROJ
    return pl.pallas_call(
        _proj_kernel,
        out_shape=jax.ShapeDtypeStruct((S, D_MODEL), F32),
        grid=(S // tm,),
        in_specs=[
            pl.BlockSpec((tm, y.shape[1]), lambda i: (i, 0)),
            _resident(w.shape),
            pl.BlockSpec((tm, D_MODEL), lambda i: (i, 0)),
            _layer_slab((1, D_MODEL), ln_row),
            _layer_slab((1, D_MODEL), ln_row),
        ],
        out_specs=pl.BlockSpec((tm, D_MODEL), lambda i: (i, 0)),
        compiler_params=_cparams("parallel"),
        name="proj_res_ln",
    )(y, w, x, g, b)


def _mlp_kernel(n_cast, x_ref, wu_ref, wd_ref, g_ref, b_ref, *rest):
    cast_src, o_ref, cast_dst = rest[:n_cast], rest[n_cast], rest[n_cast + 1:2 * n_cast + 1]
    xb_scr = rest[2 * n_cast + 1]
    j = pl.program_id(1)
    last = pl.num_programs(1) - 1

    def hidden():
        h = jnp.dot(xb_scr[...], wu_ref[...], preferred_element_type=F32)
        return jnp.square(jnp.maximum(h, 0.0)).astype(BF16)

    @pl.when(j == 0)
    def _():
        _run_casts(cast_src, cast_dst)
        xb_scr[...] = x_ref[...].astype(BF16)
        o_ref[...] = DEEPNORM_ALPHA * x_ref[...] + jnp.dot(hidden(), wd_ref[...],
                                                           preferred_element_type=F32)

    @pl.when(jnp.logical_and(j > 0, j < last))
    def _():
        _run_casts(cast_src, cast_dst)
        o_ref[...] += jnp.dot(hidden(), wd_ref[...], preferred_element_type=F32)

    @pl.when(j == last)
    def _():
        _run_casts(cast_src, cast_dst)
        h = hidden()
        chunks = [slice(r * ROWS_LN, (r + 1) * ROWS_LN) for r in range(x_ref.shape[0] // ROWS_LN)]
        for rs in chunks:
            o_ref[rs, :] += jnp.dot(h[rs, :], wd_ref[...], preferred_element_type=F32)
        for rs in chunks:
            o_ref[rs, :] = _layer_norm(o_ref[rs, :], g_ref[...], b_ref[...])


def _mlp(x, w_up, w_down, g, b, layer, casts=()):
    S = x.shape[0]
    tm, tf = TM_MLP, TF_MLP
    ni, nj = S // tm, D_FF // tf
    cast_specs = [_cast_specs(a, l, ni * nj, lambda i, j: i * nj + j) for a, l in casts]
    return pl.pallas_call(
        functools.partial(_mlp_kernel, len(casts)),
        out_shape=[jax.ShapeDtypeStruct((S, D_MODEL), F32)] + [c[2] for c in cast_specs],
        grid=(ni, nj),
        in_specs=[
            pl.BlockSpec((tm, D_MODEL), lambda i, j: (i, 0)),
            pl.BlockSpec((D_MODEL, tf), lambda i, j: (0, j)),
            pl.BlockSpec((tf, D_MODEL), lambda i, j: (j, 0)),
            _layer_slab((1, D_MODEL), 2 * layer + 1),
            _layer_slab((1, D_MODEL), 2 * layer + 1),
        ] + [c[0] for c in cast_specs],
        out_specs=[pl.BlockSpec((tm, D_MODEL), lambda i, j: (i, 0))] + [c[1] for c in cast_specs],
        scratch_shapes=[pltpu.VMEM((tm, D_MODEL), BF16)],
        compiler_params=_cparams("arbitrary", "arbitrary"),
        name="mlp_res_ln",
    )(x, w_up, w_down, g, b, *[a for a, _ in casts])


def _kv_kernel(x_ref, w_ref, b_ref, cos_ref, sin_ref, ek_ref, ev_ref, vone_ref, k_out, v_out):
    xb = x_ref[...].astype(BF16)
    kv = jnp.dot(xb, w_ref[...], preferred_element_type=F32) + b_ref[...]
    k1 = kv[:, :LANES]
    k2 = kv[:, LANES:2 * LANES]
    v = kv[:, KV_WIDTH:]
    c = cos_ref[...]
    s = sin_ref[...]
    kr = jnp.concatenate([k1 * c - k2 * s, k2 * c + k1 * s], axis=1).astype(BF16)
    kp = jnp.dot(kr, ek_ref[...], preferred_element_type=F32)
    vp = jnp.dot(v.astype(BF16), ev_ref[...], preferred_element_type=F32) + vone_ref[...]
    for h in range(N_KV_HEADS):
        for p in range(2):
            k0 = (h * 2 + p) * LANES
            k_out[h, p] = kp[:, k0:k0 + LANES].astype(BF16)
            v0 = (h * 2 + p) * 2 * LANES
            v_out[h, p] = vp[:, v0:v0 + 2 * LANES].astype(BF16)


def _kv_proj(x, w, b, cos_k, sin_k, ek, ev, vone):
    S = x.shape[0]
    tm = TM_KV
    return pl.pallas_call(
        _kv_kernel,
        out_shape=(jax.ShapeDtypeStruct((N_KV_HEADS, 2, S, LANES), BF16),
                   jax.ShapeDtypeStruct((N_KV_HEADS, 2, S, 2 * LANES), BF16)),
        grid=(S // tm,),
        in_specs=[
            pl.BlockSpec((tm, D_MODEL), lambda i: (i, 0)),
            _resident(w.shape),
            _resident(b.shape),
            pl.BlockSpec((tm, LANES), lambda i: (i, 0)),
            pl.BlockSpec((tm, LANES), lambda i: (i, 0)),
            _resident(ek.shape),
            _resident(ev.shape),
            _resident(vone.shape),
        ],
        out_specs=(pl.BlockSpec((N_KV_HEADS, 2, tm, LANES), lambda i: (0, 0, i, 0)),
                   pl.BlockSpec((N_KV_HEADS, 2, tm, 2 * LANES), lambda i: (0, 0, i, 0))),
        compiler_params=_cparams("parallel"),
        name="kv_proj",
    )(x, w, b, cos_k, sin_k, ek, ev, vone)


def _attn_kernel(sink_ref, x_ref, wq_ref, bq_ref, cos_ref, sin_ref, kp_ref, kc_ref, vp_ref,
                 vc_ref, bias_ref, eye_ref, o_ref, q_scr, k_scr, v_scr):
    tm = x_ref.shape[0]
    i = pl.program_id(0)
    xb = x_ref[...].astype(BF16)
    cos = cos_ref[...]
    sin = sin_ref[...]
    first_half = lax.broadcasted_iota(jnp.int32, (tm, LANES), 1) % HEAD_DIM < HALF
    ncol = 2 * LANES
    for c in range(D_MODEL // ncol):
        cs = slice(c * ncol, (c + 1) * ncol)
        q = jnp.dot(xb, wq_ref[:, cs], preferred_element_type=F32) + bq_ref[:, cs]
        for t in range(2):
            qt = q[:, t * LANES:(t + 1) * LANES]
            partner = jnp.where(first_half, pltpu.roll(qt, LANES - HALF, 1), pltpu.roll(qt, HALF, 1))
            qt = qt * cos + partner * sin
            q_scr[:, c * ncol + t * LANES:c * ncol + (t + 1) * LANES] = qt.astype(BF16)

    k_scr[:, :, :WINDOW, :] = kp_ref[...]
    k_scr[:, :, WINDOW:, :] = kc_ref[...]
    v_scr[:, :, :WINDOW, :] = vp_ref[...]
    v_scr[:, :, WINDOW:, :] = vc_ref[...]

    lane = lax.broadcasted_iota(jnp.int32, (WINDOW, LANES), 1)
    even_lanes = lane < HEAD_DIM
    first_tile = jnp.where(i == 0, 1, 0)
    nb = tm // WINDOW

    def stage_scores(b):
        rs = slice(b * WINDOW, (b + 1) * WINDOW)
        band = slice(b * WINDOW, b * WINDOW + BAND)
        mask_cols = bias_ref[first_tile] if b == 0 else bias_ref[0]
        scores = []
        for h in range(N_KV_HEADS):
            c0 = h * Q_PER_KV * HEAD_DIM
            lhs = jnp.concatenate(
                [q_scr[rs, c0 + j * LANES:c0 + (j + 1) * LANES] for j in range(PAIRS)], axis=0)
            kcat = jnp.concatenate([k_scr[h, 0, band, :], k_scr[h, 1, band, :]], axis=0)
            lhs = jnp.concatenate([lhs, eye_ref[...]], axis=1)
            kcat = jnp.concatenate([kcat, mask_cols], axis=1)
            scores.append(lax.dot_general(lhs, kcat, (((1,), (1,)), ((), ())),
                                          preferred_element_type=F32))
        return scores

    def stage_softmax(b, scores):
        pmats = []
        sinks = []
        for h in range(N_KV_HEADS):
            s = scores[h]
            p_rows = []
            sink_terms = []
            for j in range(PAIRS):
                p_cols = []
                sink_arg = []
                for p in range(2):
                    sub = s[j * WINDOW:(j + 1) * WINDOW, p * BAND:(p + 1) * BAND]
                    sk = sink_ref[h * Q_PER_KV + 2 * j + p]
                    m = jnp.maximum(jnp.max(sub, axis=1, keepdims=True), sk)
                    p_cols.append(jnp.exp2(sub - m).astype(BF16))
                    sink_arg.append(sk - m)
                p_rows.append(jnp.concatenate(p_cols, axis=1))
                sink_terms.append(jnp.exp2(jnp.where(even_lanes, sink_arg[0], sink_arg[1])))
            pmats.append(jnp.concatenate(p_rows, axis=0))
            sinks.append(sink_terms)
        return pmats, sinks

    def stage_values(b, pmats, sinks):
        rs = slice(b * WINDOW, (b + 1) * WINDOW)
        band = slice(b * WINDOW, b * WINDOW + BAND)
        for h in range(N_KV_HEADS):
            c0 = h * Q_PER_KV * HEAD_DIM
            vcat = jnp.concatenate([v_scr[h, 0, band, :], v_scr[h, 1, band, :]], axis=0)
            oa = jnp.dot(pmats[h], vcat, preferred_element_type=F32)
            for j in range(PAIRS):
                js = slice(j * WINDOW, (j + 1) * WINDOW)
                den = oa[js, LANES:] + sinks[h][j]
                o_ref[rs, c0 + j * LANES:c0 + (j + 1) * LANES] = (oa[js, :LANES] / den).astype(BF16)

    scores = stage_scores(0)
    for b in range(nb):
        nxt = stage_scores(b + 1) if b + 1 < nb else None
        pmats, sinks = stage_softmax(b, scores)
        stage_values(b, pmats, sinks)
        scores = nxt


def _attention(sinks, x, w_q, b_q, cos_q, sin_q, kmat, vmat, bias, eye, layer):
    S = x.shape[0]
    tm = TM_ATT
    nb = tm // WINDOW
    prev = lambda i, s: (0, 0, jnp.maximum(i * nb - 1, 0), 0)
    cur = lambda i, s: (0, 0, i, 0)
    grid_spec = pltpu.PrefetchScalarGridSpec(
        num_scalar_prefetch=1,
        grid=(S // tm,),
        in_specs=[
            pl.BlockSpec((tm, D_MODEL), lambda i, s: (i, 0)),
            _resident(w_q.shape),
            _layer_slab(b_q.shape[1:], layer),
            pl.BlockSpec((tm, LANES), lambda i, s: (i, 0)),
            pl.BlockSpec((tm, LANES), lambda i, s: (i, 0)),
            pl.BlockSpec((N_KV_HEADS, 2, WINDOW, LANES), prev),
            pl.BlockSpec((N_KV_HEADS, 2, tm, LANES), cur),
            pl.BlockSpec((N_KV_HEADS, 2, WINDOW, 2 * LANES), prev),
            pl.BlockSpec((N_KV_HEADS, 2, tm, 2 * LANES), cur),
            _resident(bias.shape),
            _resident(eye.shape),
        ],
        out_specs=pl.BlockSpec((tm, D_MODEL), lambda i, s: (i, 0)),
        scratch_shapes=[
            pltpu.VMEM((tm, D_MODEL), BF16),
            pltpu.VMEM((N_KV_HEADS, 2, tm + WINDOW, LANES), BF16),
            pltpu.VMEM((N_KV_HEADS, 2, tm + WINDOW, 2 * LANES), BF16),
        ],
    )
    return pl.pallas_call(
        _attn_kernel,
        out_shape=jax.ShapeDtypeStruct((S, D_MODEL), BF16),
        grid_spec=grid_spec,
        compiler_params=_cparams("parallel"),
        name="swa_attention",
    )(sinks, x, w_q, b_q, cos_q, sin_q, kmat, kmat, vmat, vmat, bias, eye)


def _k_perm():
    perm = np.zeros(KV_WIDTH, np.int32)
    for half in range(2):
        for h in range(N_KV_HEADS):
            for d in range(HALF):
                perm[half * LANES + h * HALF + d] = h * HEAD_DIM + half * HALF + d
    return perm


def _placements():
    ek = np.zeros((KV_WIDTH, N_KV_HEADS * 2 * LANES), np.float32)
    ev = np.zeros((KV_WIDTH, N_KV_HEADS * 2 * 2 * LANES), np.float32)
    vone = np.zeros((1, N_KV_HEADS * 2 * 2 * LANES), np.float32)
    for h in range(N_KV_HEADS):
        for p in range(2):
            for half in range(2):
                for d in range(HALF):
                    ek[half * LANES + h * HALF + d,
                       (h * 2 + p) * LANES + p * HEAD_DIM + half * HALF + d] = 1.0
            for d in range(HEAD_DIM):
                base = (h * 2 + p) * 2 * LANES
                ev[h * HEAD_DIM + d, base + p * HEAD_DIM + d] = 1.0
                vone[0, base + LANES + p * HEAD_DIM + d] = 1.0
    return ek, ev, vone


def _band_mask():
    i = np.arange(WINDOW)[None, :]
    j = np.arange(BAND)[:, None]
    in_band = (j > i) & (j <= i + WINDOW)
    mask = np.zeros((2, BAND, WINDOW), np.float32)
    mask[0] = np.where(in_band, 0.0, NEG_BF16)
    mask[1] = np.where(in_band & (j >= WINDOW), 0.0, NEG_BF16)
    eye = np.tile(np.eye(WINDOW, dtype=np.float32), (PAIRS, 1))
    return np.tile(mask, (1, 2, 1)), eye


def kernel(x, a_w_in, a_b_in, a_ln_v_g, a_ln_v_b, a_w_s, a_b_s, a_w_out, kv_w, kv_b, b_w_q,
           b_b_q, b_sinks, b_w_o, mlp_w_up, mlp_w_down, ln_g, ln_b):
    B, S, _ = x.shape
    assert B == 1 and S % TM_MLP == 0
    h = x.reshape(S, D_MODEL)

    inv_freq = jnp.tile(ROPE_THETA ** (-jnp.arange(0, HEAD_DIM, 2, dtype=F32) / HEAD_DIM), 4)
    blk = (jnp.arange(S // WINDOW, dtype=jnp.int32) * WINDOW).astype(F32)[:, None] * inv_freq[None, :]
    off = jnp.arange(WINDOW, dtype=jnp.int32).astype(F32)[:, None] * inv_freq[None, :]
    cb, sb = jnp.cos(blk)[:, None, :], jnp.sin(blk)[:, None, :]
    co, so = jnp.cos(off)[None, :, :], jnp.sin(off)[None, :, :]
    cos_k = (cb * co - sb * so).reshape(S, LANES)
    sin_k = (sb * co + cb * so).reshape(S, LANES)
    scale = HEAD_DIM ** -0.5 * LOG2_E
    rot_sign = np.tile(np.repeat(np.array([-scale, scale], np.float32), HALF), 2)
    cos_q = cos_k * scale
    sin_q = sin_k * rot_sign[None, :]

    k_perm = _k_perm()
    ek, ev, vone = _placements()
    ek = jnp.asarray(ek, BF16)
    ev = jnp.asarray(ev, BF16)
    vone = jnp.asarray(vone, F32)
    mask, eye = _band_mask()
    mask = jnp.asarray(mask, BF16)
    eye = jnp.asarray(eye, BF16)

    rows = lambda v: v.reshape(-1, 1, v.shape[-1])
    b_in, ln_v_g, ln_v_b = rows(a_b_in), rows(a_ln_v_g), rows(a_ln_v_b)
    b_q = rows(b_b_q)
    g_rows, b_rows = rows(ln_g), rows(ln_b)
    bs = jnp.broadcast_to(a_b_s[:, :, :, None], (N_A_LAYERS, A_GROUPS, CHUNK, A_GROUP_DIM))

    def mixer_casts(layer):
        if layer < N_A_LAYERS:
            return [(a_w_in, layer), (a_w_out, layer)]
        return [(b_w_q, layer - N_A_LAYERS), (b_w_o, layer - N_A_LAYERS)]

    def layer_casts(layer):
        return mixer_casts(layer) + [(mlp_w_up, layer), (mlp_w_down, layer)]

    w_a = a_w_in[0].astype(BF16)
    for layer in range(DEPTH):
        nxt = layer_casts(layer + 1) if layer + 1 < DEPTH else []
        if layer < N_A_LAYERS:
            first = [(a_w_out, 0), (mlp_w_up, 0), (mlp_w_down, 0)] if layer == 0 else []
            y, *done = _sgu(h, w_a, b_in, ln_v_g, ln_v_b, a_w_s, bs, layer, first)
            if layer == 0:
                w_b, w_up, w_down = done
        else:
            if layer == N_A_LAYERS:
                w_kv = jnp.concatenate([kv_w[:, :KV_WIDTH][:, k_perm], kv_w[:, KV_WIDTH:]], axis=1)
                b_kv = jnp.concatenate([kv_b[:KV_WIDTH][k_perm], kv_b[KV_WIDTH:]])
                kmat, vmat = _kv_proj(h, w_kv.astype(BF16), b_kv.reshape(1, -1), cos_k, sin_k,
                                      ek, ev, vone)
            j = layer - N_A_LAYERS
            y = _attention(b_sinks[j] * LOG2_E, h, w_a, b_q, cos_q, sin_q, kmat, vmat, mask, eye, j)
        h = _proj_res_ln(y, w_b, h, g_rows, b_rows, 2 * layer)
        h, *done = _mlp(h, w_up, w_down, g_rows, b_rows, layer, nxt)
        if nxt:
            w_a, w_b, w_up, w_down = done
    return h.reshape(B, S, D_MODEL)
```

```python
import functools

import numpy as np
import jax
import jax.numpy as jnp
from jax import lax
from jax.experimental import pallas as pl
from jax.experimental.pallas import tpu as pltpu

D_MODEL = 2048
DEPTH = 4
N_A_LAYERS = DEPTH // 2
CHUNK = 128
A_WIDTH = D_MODEL
A_GROUPS = 8
A_GROUP_DIM = A_WIDTH // A_GROUPS
HEAD_DIM = 64
HALF = HEAD_DIM // 2
N_Q_HEADS = D_MODEL // HEAD_DIM
N_KV_HEADS = 4
Q_PER_KV = N_Q_HEADS // N_KV_HEADS
PAIRS = Q_PER_KV // 2
WINDOW = 128
BAND = 2 * WINDOW
ROPE_THETA = 10000.0
D_FF = 4 * D_MODEL
LN_EPS = 1e-5
DEEPNORM_ALPHA = (2.0 * DEPTH) ** 0.25
KV_WIDTH = N_KV_HEADS * HEAD_DIM

LANES = 128
V7X_VMEM_LIMIT = 56 * 1024 * 1024

BF16 = jnp.bfloat16
F32 = jnp.float32
NEG_BF16 = float(jnp.finfo(jnp.bfloat16).min)
LOG2_E = float(np.log2(np.e))

TM_SGU = 512
TM_PROJ = 1024
TM_MLP = 1024
TF_MLP = 512
TM_KV = 1024
TM_ATT = 512
ROWS_LN = 256
ROWS_PROJ = 256


def _cparams(*sem):
    return pltpu.CompilerParams(dimension_semantics=sem, vmem_limit_bytes=V7X_VMEM_LIMIT)


def _resident(shape):
    nd = len(shape)
    return pl.BlockSpec(shape, lambda *_: (0,) * nd, pipeline_mode=pl.Buffered(1))


def _layer_slab(tail, layer):
    nd = len(tail)
    return pl.BlockSpec((None,) + tuple(tail), lambda *_: (layer,) + (0,) * nd,
                        pipeline_mode=pl.Buffered(1))


BF16_SUBLANES = 16


def _cast_specs(stacked, layer, nsteps, step_of):
    _, nrows, ncols = stacked.shape
    nblk = nsteps
    while nrows % nblk or (nrows // nblk) % BF16_SUBLANES:
        nblk //= 2
    rb, group = nrows // nblk, nsteps // nblk
    in_spec = pl.BlockSpec((None, rb, ncols), lambda *ids: (layer, step_of(*ids) // group, 0))
    out_spec = pl.BlockSpec((rb, ncols), lambda *ids: (step_of(*ids) // group, 0))
    return in_spec, out_spec, jax.ShapeDtypeStruct((nrows, ncols), BF16)


def _run_casts(srcs, dsts):
    for src, dst in zip(srcs, dsts):
        dst[...] = src[...].astype(BF16)


def _gelu(t):
    return 0.5 * t * (1.0 + lax.erf(t * np.float32(np.sqrt(0.5))))


def _layer_norm(t, g, b):
    mu = jnp.mean(t, axis=-1, keepdims=True)
    c = t - mu
    var = jnp.mean(c * c, axis=-1, keepdims=True)
    return c * lax.rsqrt(var + LN_EPS) * g + b


def _sgu_kernel(n_cast, x_ref, win_ref, bin_ref, g_ref, b_ref, ws_ref, bs_ref, *rest):
    cast_src, y_ref, cast_dst = rest[:n_cast], rest[n_cast], rest[n_cast + 1:2 * n_cast + 1]
    (v_scr,) = rest[2 * n_cast + 1:]
    tm = x_ref.shape[0]
    xb = x_ref[...].astype(BF16)

    def gelu_proj(cs):
        z = jnp.dot(xb, win_ref[:, cs], preferred_element_type=F32) + bin_ref[:, cs]
        return _gelu(z)

    ncol = 512
    for c in range(A_WIDTH // ncol):
        v_scr[:, c * ncol:(c + 1) * ncol] = gelu_proj(slice(A_WIDTH + c * ncol, A_WIDTH + (c + 1) * ncol))
    groups = [slice(g * A_GROUP_DIM, (g + 1) * A_GROUP_DIM) for g in range(A_GROUPS)]
    u_next = gelu_proj(groups[0])
    v_scr[...] = _layer_norm(v_scr[...], g_ref[...], b_ref[...])
    row = lax.broadcasted_iota(jnp.int32, (CHUNK, CHUNK), 0)
    col = lax.broadcasted_iota(jnp.int32, (CHUNK, CHUNK), 1)
    causal = col <= row
    for g, gs in enumerate(groups):
        u = u_next
        if g + 1 < A_GROUPS:
            u_next = gelu_proj(groups[g + 1])
        wsg = jnp.where(causal, ws_ref[g], 0.0).astype(BF16)
        bsg = bs_ref[g]
        for c in range(tm // CHUNK):
            rs = slice(c * CHUNK, (c + 1) * CHUNK)
            s = jnp.dot(wsg, v_scr[rs, gs].astype(BF16), preferred_element_type=F32) + bsg
            y_ref[rs, gs] = (u[rs, :] * s).astype(BF16)
    _run_casts(cast_src, cast_dst)


def _sgu(x, w_in, b_in, g, b, w_s, b_s, layer, casts=()):
    S = x.shape[0]
    tm = TM_SGU
    nsteps = S // tm
    cast_specs = [_cast_specs(a, l, nsteps, lambda i: i) for a, l in casts]
    return pl.pallas_call(
        functools.partial(_sgu_kernel, len(casts)),
        out_shape=[jax.ShapeDtypeStruct((S, A_WIDTH), BF16)] + [c[2] for c in cast_specs],
        grid=(nsteps,),
        in_specs=[
            pl.BlockSpec((tm, D_MODEL), lambda i: (i, 0)),
            _resident(w_in.shape),
            _layer_slab((1, 2 * A_WIDTH), layer),
            _layer_slab((1, A_WIDTH), layer),
            _layer_slab((1, A_WIDTH), layer),
            _layer_slab((A_GROUPS, CHUNK, CHUNK), layer),
            _layer_slab((A_GROUPS, CHUNK, A_GROUP_DIM), layer),
        ] + [c[0] for c in cast_specs],
        out_specs=[pl.BlockSpec((tm, A_WIDTH), lambda i: (i, 0))] + [c[1] for c in cast_specs],
        scratch_shapes=[pltpu.VMEM((tm, A_WIDTH), F32)],
        compiler_params=_cparams("arbitrary"),
        name="sgu_mix",
    )(x, w_in, b_in, g, b, w_s, b_s, *[a for a, _ in casts])


def _proj_kernel(y_ref, w_ref, x_ref, g_ref, b_ref, o_ref):
    tm = x_ref.shape[0]
    for r in range(tm // ROWS_PROJ):
        rs = slice(r * ROWS_PROJ, (r + 1) * ROWS_PROJ)
        mix = jnp.dot(y_ref[rs, :], w_ref[...], preferred_element_type=F32)
        o_ref[rs, :] = _layer_norm(DEEPNORM_ALPHA * x_ref[rs, :] + mix, g_ref[...], b_ref[...])


def _proj_res_ln(y, w, x, g, b, ln_row):
    S = x.shape[0]
    tm = TM_PROJ
    return pl.pallas_call(
        _proj_kernel,
        out_shape=jax.ShapeDtypeStruct((S, D_MODEL), F32),
        grid=(S // tm,),
        in_specs=[
            pl.BlockSpec((tm, y.shape[1]), lambda i: (i, 0)),
            _resident(w.shape),
            pl.BlockSpec((tm, D_MODEL), lambda i: (i, 0)),
            _layer_slab((1, D_MODEL), ln_row),
            _layer_slab((1, D_MODEL), ln_row),
        ],
        out_specs=pl.BlockSpec((tm, D_MODEL), lambda i: (i, 0)),
        compiler_params=_cparams("parallel"),
        name="proj_res_ln",
    )(y, w, x, g, b)


def _mlp_kernel(n_cast, x_ref, wu_ref, wd_ref, g_ref, b_ref, *rest):
    cast_src, o_ref, cast_dst = rest[:n_cast], rest[n_cast], rest[n_cast + 1:2 * n_cast + 1]
    xb_scr = rest[2 * n_cast + 1]
    j = pl.program_id(1)
    last = pl.num_programs(1) - 1

    def hidden():
        h = jnp.dot(xb_scr[...], wu_ref[...], preferred_element_type=F32)
        return jnp.square(jnp.maximum(h, 0.0)).astype(BF16)

    @pl.when(j == 0)
    def _():
        xb_scr[...] = x_ref[...].astype(BF16)
        o_ref[...] = DEEPNORM_ALPHA * x_ref[...] + jnp.dot(hidden(), wd_ref[...],
                                                           preferred_element_type=F32)
        _run_casts(cast_src, cast_dst)

    @pl.when(jnp.logical_and(j > 0, j < last))
    def _():
        o_ref[...] += jnp.dot(hidden(), wd_ref[...], preferred_element_type=F32)
        _run_casts(cast_src, cast_dst)

    @pl.when(j == last)
    def _():
        h = hidden()
        chunks = [slice(r * ROWS_LN, (r + 1) * ROWS_LN) for r in range(x_ref.shape[0] // ROWS_LN)]
        for rs in chunks:
            o_ref[rs, :] += jnp.dot(h[rs, :], wd_ref[...], preferred_element_type=F32)
        for rs in chunks:
            o_ref[rs, :] = _layer_norm(o_ref[rs, :], g_ref[...], b_ref[...])
        _run_casts(cast_src, cast_dst)


def _mlp(x, w_up, w_down, g, b, layer, casts=()):
    S = x.shape[0]
    tm, tf = TM_MLP, TF_MLP
    ni, nj = S // tm, D_FF // tf
    cast_specs = [_cast_specs(a, l, ni * nj, lambda i, j: i * nj + j) for a, l in casts]
    return pl.pallas_call(
        functools.partial(_mlp_kernel, len(casts)),
        out_shape=[jax.ShapeDtypeStruct((S, D_MODEL), F32)] + [c[2] for c in cast_specs],
        grid=(ni, nj),
        in_specs=[
            pl.BlockSpec((tm, D_MODEL), lambda i, j: (i, 0)),
            pl.BlockSpec((D_MODEL, tf), lambda i, j: (0, j)),
            pl.BlockSpec((tf, D_MODEL), lambda i, j: (j, 0)),
            _layer_slab((1, D_MODEL), 2 * layer + 1),
            _layer_slab((1, D_MODEL), 2 * layer + 1),
        ] + [c[0] for c in cast_specs],
        out_specs=[pl.BlockSpec((tm, D_MODEL), lambda i, j: (i, 0))] + [c[1] for c in cast_specs],
        scratch_shapes=[pltpu.VMEM((tm, D_MODEL), BF16)],
        compiler_params=_cparams("arbitrary", "arbitrary"),
        name="mlp_res_ln",
    )(x, w_up, w_down, g, b, *[a for a, _ in casts])


def _kv_kernel(x_ref, w_ref, b_ref, cos_ref, sin_ref, ek_ref, ev_ref, vone_ref, k_out, v_out):
    xb = x_ref[...].astype(BF16)
    kv = jnp.dot(xb, w_ref[...], preferred_element_type=F32) + b_ref[...]
    k1 = kv[:, :LANES]
    k2 = kv[:, LANES:2 * LANES]
    v = kv[:, KV_WIDTH:]
    c = cos_ref[...]
    s = sin_ref[...]
    kr = jnp.concatenate([k1 * c - k2 * s, k2 * c + k1 * s], axis=1).astype(BF16)
    kp = jnp.dot(kr, ek_ref[...], preferred_element_type=F32)
    vp = jnp.dot(v.astype(BF16), ev_ref[...], preferred_element_type=F32) + vone_ref[...]
    for h in range(N_KV_HEADS):
        for p in range(2):
            k0 = (h * 2 + p) * LANES
            k_out[h, p] = kp[:, k0:k0 + LANES].astype(BF16)
            v0 = (h * 2 + p) * 2 * LANES
            v_out[h, p] = vp[:, v0:v0 + 2 * LANES].astype(BF16)


def _kv_proj(x, w, b, cos_k, sin_k, ek, ev, vone):
    S = x.shape[0]
    tm = TM_KV
    return pl.pallas_call(
        _kv_kernel,
        out_shape=(jax.ShapeDtypeStruct((N_KV_HEADS, 2, S, LANES), BF16),
                   jax.ShapeDtypeStruct((N_KV_HEADS, 2, S, 2 * LANES), BF16)),
        grid=(S // tm,),
        in_specs=[
            pl.BlockSpec((tm, D_MODEL), lambda i: (i, 0)),
            _resident(w.shape),
            _resident(b.shape),
            pl.BlockSpec((tm, LANES), lambda i: (i, 0)),
            pl.BlockSpec((tm, LANES), lambda i: (i, 0)),
            _resident(ek.shape),
            _resident(ev.shape),
            _resident(vone.shape),
        ],
        out_specs=(pl.BlockSpec((N_KV_HEADS, 2, tm, LANES), lambda i: (0, 0, i, 0)),
                   pl.BlockSpec((N_KV_HEADS, 2, tm, 2 * LANES), lambda i: (0, 0, i, 0))),
        compiler_params=_cparams("parallel"),
        name="kv_proj",
    )(x, w, b, cos_k, sin_k, ek, ev, vone)


def _attn_kernel(sink_ref, x_ref, wq_ref, bq_ref, cos_ref, sin_ref, kp_ref, kc_ref, vp_ref,
                 vc_ref, bias_ref, eye_ref, o_ref, q_scr, k_scr, v_scr):
    tm = x_ref.shape[0]
    i = pl.program_id(0)
    xb = x_ref[...].astype(BF16)
    cos = cos_ref[...]
    sin = sin_ref[...]
    first_half = lax.broadcasted_iota(jnp.int32, (tm, LANES), 1) % HEAD_DIM < HALF
    ncol = 2 * LANES
    for c in range(D_MODEL // ncol):
        cs = slice(c * ncol, (c + 1) * ncol)
        q = jnp.dot(xb, wq_ref[:, cs], preferred_element_type=F32) + bq_ref[:, cs]
        for t in range(2):
            qt = q[:, t * LANES:(t + 1) * LANES]
            partner = jnp.where(first_half, pltpu.roll(qt, LANES - HALF, 1), pltpu.roll(qt, HALF, 1))
            qt = qt * cos + partner * sin
            q_scr[:, c * ncol + t * LANES:c * ncol + (t + 1) * LANES] = qt.astype(BF16)

    k_scr[:, :, :WINDOW, :] = kp_ref[...]
    k_scr[:, :, WINDOW:, :] = kc_ref[...]
    v_scr[:, :, :WINDOW, :] = vp_ref[...]
    v_scr[:, :, WINDOW:, :] = vc_ref[...]

    lane = lax.broadcasted_iota(jnp.int32, (WINDOW, LANES), 1)
    even_lanes = lane < HEAD_DIM
    first_tile = jnp.where(i == 0, 1, 0)
    nb = tm // WINDOW

    def stage_scores(b):
        rs = slice(b * WINDOW, (b + 1) * WINDOW)
        band = slice(b * WINDOW, b * WINDOW + BAND)
        mask_cols = bias_ref[first_tile] if b == 0 else bias_ref[0]
        scores = []
        for h in range(N_KV_HEADS):
            c0 = h * Q_PER_KV * HEAD_DIM
            lhs = jnp.concatenate(
                [q_scr[rs, c0 + j * LANES:c0 + (j + 1) * LANES] for j in range(PAIRS)], axis=0)
            kcat = jnp.concatenate([k_scr[h, 0, band, :], k_scr[h, 1, band, :]], axis=0)
            lhs = jnp.concatenate([lhs, eye_ref[...]], axis=1)
            kcat = jnp.concatenate([kcat, mask_cols], axis=1)
            scores.append(lax.dot_general(lhs, kcat, (((1,), (1,)), ((), ())),
                                          preferred_element_type=F32))
        return scores

    def stage_softmax(b, scores):
        pmats = []
        sinks = []
        for h in range(N_KV_HEADS):
            s = scores[h]
            p_rows = []
            sink_terms = []
            for j in range(PAIRS):
                p_cols = []
                sink_arg = []
                for p in range(2):
                    sub = s[j * WINDOW:(j + 1) * WINDOW, p * BAND:(p + 1) * BAND]
                    sk = sink_ref[h * Q_PER_KV + 2 * j + p]
                    m = jnp.maximum(jnp.max(sub, axis=1, keepdims=True), sk)
                    p_cols.append(jnp.exp2(sub - m).astype(BF16))
                    sink_arg.append(sk - m)
                p_rows.append(jnp.concatenate(p_cols, axis=1))
                sink_terms.append(jnp.exp2(jnp.where(even_lanes, sink_arg[0], sink_arg[1])))
            pmats.append(jnp.concatenate(p_rows, axis=0))
            sinks.append(sink_terms)
        return pmats, sinks

    def stage_values(b, pmats, sinks):
        rs = slice(b * WINDOW, (b + 1) * WINDOW)
        band = slice(b * WINDOW, b * WINDOW + BAND)
        for h in range(N_KV_HEADS):
            c0 = h * Q_PER_KV * HEAD_DIM
            vcat = jnp.concatenate([v_scr[h, 0, band, :], v_scr[h, 1, band, :]], axis=0)
            oa = jnp.dot(pmats[h], vcat, preferred_element_type=F32)
            for j in range(PAIRS):
                js = slice(j * WINDOW, (j + 1) * WINDOW)
                den = oa[js, LANES:] + sinks[h][j]
                o_ref[rs, c0 + j * LANES:c0 + (j + 1) * LANES] = (oa[js, :LANES] / den).astype(BF16)

    scores = stage_scores(0)
    for b in range(nb):
        nxt = stage_scores(b + 1) if b + 1 < nb else None
        pmats, sinks = stage_softmax(b, scores)
        stage_values(b, pmats, sinks)
        scores = nxt


def _attention(sinks, x, w_q, b_q, cos_q, sin_q, kmat, vmat, bias, eye, layer):
    S = x.shape[0]
    tm = TM_ATT
    nb = tm // WINDOW
    prev = lambda i, s: (0, 0, jnp.maximum(i * nb - 1, 0), 0)
    cur = lambda i, s: (0, 0, i, 0)
    grid_spec = pltpu.PrefetchScalarGridSpec(
        num_scalar_prefetch=1,
        grid=(S // tm,),
        in_specs=[
            pl.BlockSpec((tm, D_MODEL), lambda i, s: (i, 0)),
            _resident(w_q.shape),
            _layer_slab(b_q.shape[1:], layer),
            pl.BlockSpec((tm, LANES), lambda i, s: (i, 0)),
            pl.BlockSpec((tm, LANES), lambda i, s: (i, 0)),
            pl.BlockSpec((N_KV_HEADS, 2, WINDOW, LANES), prev),
            pl.BlockSpec((N_KV_HEADS, 2, tm, LANES), cur),
            pl.BlockSpec((N_KV_HEADS, 2, WINDOW, 2 * LANES), prev),
            pl.BlockSpec((N_KV_HEADS, 2, tm, 2 * LANES), cur),
            _resident(bias.shape),
            _resident(eye.shape),
        ],
        out_specs=pl.BlockSpec((tm, D_MODEL), lambda i, s: (i, 0)),
        scratch_shapes=[
            pltpu.VMEM((tm, D_MODEL), BF16),
            pltpu.VMEM((N_KV_HEADS, 2, tm + WINDOW, LANES), BF16),
            pltpu.VMEM((N_KV_HEADS, 2, tm + WINDOW, 2 * LANES), BF16),
        ],
    )
    return pl.pallas_call(
        _attn_kernel,
        out_shape=jax.ShapeDtypeStruct((S, D_MODEL), BF16),
        grid_spec=grid_spec,
        compiler_params=_cparams("parallel"),
        name="swa_attention",
    )(sinks, x, w_q, b_q, cos_q, sin_q, kmat, kmat, vmat, vmat, bias, eye)


def _k_perm():
    perm = np.zeros(KV_WIDTH, np.int32)
    for half in range(2):
        for h in range(N_KV_HEADS):
            for d in range(HALF):
                perm[half * LANES + h * HALF + d] = h * HEAD_DIM + half * HALF + d
    return perm


def _placements():
    ek = np.zeros((KV_WIDTH, N_KV_HEADS * 2 * LANES), np.float32)
    ev = np.zeros((KV_WIDTH, N_KV_HEADS * 2 * 2 * LANES), np.float32)
    vone = np.zeros((1, N_KV_HEADS * 2 * 2 * LANES), np.float32)
    for h in range(N_KV_HEADS):
        for p in range(2):
            for half in range(2):
                for d in range(HALF):
                    ek[half * LANES + h * HALF + d,
                       (h * 2 + p) * LANES + p * HEAD_DIM + half * HALF + d] = 1.0
            for d in range(HEAD_DIM):
                base = (h * 2 + p) * 2 * LANES
                ev[h * HEAD_DIM + d, base + p * HEAD_DIM + d] = 1.0
                vone[0, base + LANES + p * HEAD_DIM + d] = 1.0
    return ek, ev, vone


def _band_mask():
    i = np.arange(WINDOW)[None, :]
    j = np.arange(BAND)[:, None]
    in_band = (j > i) & (j <= i + WINDOW)
    mask = np.zeros((2, BAND, WINDOW), np.float32)
    mask[0] = np.where(in_band, 0.0, NEG_BF16)
    mask[1] = np.where(in_band & (j >= WINDOW), 0.0, NEG_BF16)
    eye = np.tile(np.eye(WINDOW, dtype=np.float32), (PAIRS, 1))
    return np.tile(mask, (1, 2, 1)), eye


def kernel(x, a_w_in, a_b_in, a_ln_v_g, a_ln_v_b, a_w_s, a_b_s, a_w_out, kv_w, kv_b, b_w_q,
           b_b_q, b_sinks, b_w_o, mlp_w_up, mlp_w_down, ln_g, ln_b):
    B, S, _ = x.shape
    assert B == 1 and S % TM_MLP == 0
    h = x.reshape(S, D_MODEL)

    inv_freq = jnp.tile(ROPE_THETA ** (-jnp.arange(0, HEAD_DIM, 2, dtype=F32) / HEAD_DIM), 4)
    blk = (jnp.arange(S // WINDOW, dtype=jnp.int32) * WINDOW).astype(F32)[:, None] * inv_freq[None, :]
    off = jnp.arange(WINDOW, dtype=jnp.int32).astype(F32)[:, None] * inv_freq[None, :]
    cb, sb = jnp.cos(blk)[:, None, :], jnp.sin(blk)[:, None, :]
    co, so = jnp.cos(off)[None, :, :], jnp.sin(off)[None, :, :]
    cos_k = (cb * co - sb * so).reshape(S, LANES)
    sin_k = (sb * co + cb * so).reshape(S, LANES)
    scale = HEAD_DIM ** -0.5 * LOG2_E
    rot_sign = np.tile(np.repeat(np.array([-scale, scale], np.float32), HALF), 2)
    cos_q = cos_k * scale
    sin_q = sin_k * rot_sign[None, :]

    k_perm = _k_perm()
    ek, ev, vone = _placements()
    ek = jnp.asarray(ek, BF16)
    ev = jnp.asarray(ev, BF16)
    vone = jnp.asarray(vone, F32)
    mask, eye = _band_mask()
    mask = jnp.asarray(mask, BF16)
    eye = jnp.asarray(eye, BF16)

    rows = lambda v: v.reshape(-1, 1, v.shape[-1])
    b_in, ln_v_g, ln_v_b = rows(a_b_in), rows(a_ln_v_g), rows(a_ln_v_b)
    b_q = rows(b_b_q)
    g_rows, b_rows = rows(ln_g), rows(ln_b)
    bs = jnp.broadcast_to(a_b_s[:, :, :, None], (N_A_LAYERS, A_GROUPS, CHUNK, A_GROUP_DIM))

    def mixer_casts(layer):
        if layer < N_A_LAYERS:
            return [(a_w_in, layer), (a_w_out, layer)]
        return [(b_w_q, layer - N_A_LAYERS), (b_w_o, layer - N_A_LAYERS)]

    def layer_casts(layer):
        return mixer_casts(layer) + [(mlp_w_up, layer), (mlp_w_down, layer)]

    w_a = a_w_in[0].astype(BF16)
    for layer in range(DEPTH):
        nxt = layer_casts(layer + 1) if layer + 1 < DEPTH else []
        if layer < N_A_LAYERS:
            first = [(a_w_out, 0), (mlp_w_up, 0), (mlp_w_down, 0)] if layer == 0 else []
            y, *done = _sgu(h, w_a, b_in, ln_v_g, ln_v_b, a_w_s, bs, layer, first)
            if layer == 0:
                w_b, w_up, w_down = done
        else:
            if layer == N_A_LAYERS:
                w_kv = jnp.concatenate([kv_w[:, :KV_WIDTH][:, k_perm], kv_w[:, KV_WIDTH:]], axis=1)
                b_kv = jnp.concatenate([kv_b[:KV_WIDTH][k_perm], kv_b[KV_WIDTH:]])
                kmat, vmat = _kv_proj(h, w_kv.astype(BF16), b_kv.reshape(1, -1), cos_k, sin_k,
                                      ek, ev, vone)
            j = layer - N_A_LAYERS
            y = _attention(b_sinks[j] * LOG2_E, h, w_a, b_q, cos_q, sin_q, kmat, vmat, mask, eye, j)
        h = _proj_res_ln(y, w_b, h, g_rows, b_rows, 2 * layer)
        h, *done = _mlp(h, w_up, w_down, g_rows, b_rows, layer, nxt)
        if nxt:
            w_a, w_b, w_up, w_down = done
    return h.reshape(B, S, D_MODEL)
```

```python
import functools

import numpy as np
import jax
import jax.numpy as jnp
from jax import lax
from jax.experimental import pallas as pl
from jax.experimental.pallas import tpu as pltpu

D_MODEL = 2048
DEPTH = 4
N_A_LAYERS = DEPTH // 2
CHUNK = 128
A_WIDTH = D_MODEL
A_GROUPS = 8
A_GROUP_DIM = A_WIDTH // A_GROUPS
HEAD_DIM = 64
HALF = HEAD_DIM // 2
N_Q_HEADS = D_MODEL // HEAD_DIM
N_KV_HEADS = 4
Q_PER_KV = N_Q_HEADS // N_KV_HEADS
PAIRS = Q_PER_KV // 2
WINDOW = 128
BAND = 2 * WINDOW
ROPE_THETA = 10000.0
D_FF = 4 * D_MODEL
LN_EPS = 1e-5
DEEPNORM_ALPHA = (2.0 * DEPTH) ** 0.25
KV_WIDTH = N_KV_HEADS * HEAD_DIM

LANES = 128
V7X_VMEM_LIMIT = 56 * 1024 * 1024

BF16 = jnp.bfloat16
F32 = jnp.float32
NEG_BF16 = float(jnp.finfo(jnp.bfloat16).min)
LOG2_E = float(np.log2(np.e))

TM_SGU = 512
TM_PROJ = 1024
TM_MLP = 1024
TF_MLP = 512
TM_KV = 1024
TM_ATT = 512
ROWS_LN = 256
ROWS_PROJ = 256


def _cparams(*sem):
    return pltpu.CompilerParams(dimension_semantics=sem, vmem_limit_bytes=V7X_VMEM_LIMIT)


def _resident(shape):
    nd = len(shape)
    return pl.BlockSpec(shape, lambda *_: (0,) * nd, pipeline_mode=pl.Buffered(1))


def _layer_slab(tail, layer):
    nd = len(tail)
    return pl.BlockSpec((None,) + tuple(tail), lambda *_: (layer,) + (0,) * nd,
                        pipeline_mode=pl.Buffered(1))


BF16_SUBLANES = 16


def _cast_specs(stacked, layer, nsteps, step_of):
    _, nrows, ncols = stacked.shape
    nblk = nsteps
    while nrows % nblk or (nrows // nblk) % BF16_SUBLANES:
        nblk //= 2
    rb, group = nrows // nblk, nsteps // nblk
    in_spec = pl.BlockSpec((None, rb, ncols), lambda *ids: (layer, step_of(*ids) // group, 0))
    out_spec = pl.BlockSpec((rb, ncols), lambda *ids: (step_of(*ids) // group, 0))
    return in_spec, out_spec, jax.ShapeDtypeStruct((nrows, ncols), BF16)


def _run_casts(srcs, dsts):
    for src, dst in zip(srcs, dsts):
        dst[...] = src[...].astype(BF16)


def _gelu(t):
    return 0.5 * t * (1.0 + lax.erf(t * np.float32(np.sqrt(0.5))))


def _layer_norm(t, g, b):
    mu = jnp.mean(t, axis=-1, keepdims=True)
    c = t - mu
    var = jnp.mean(c * c, axis=-1, keepdims=True)
    return c * lax.rsqrt(var + LN_EPS) * g + b


def _sgu_kernel(n_cast, x_ref, win_ref, bin_ref, g_ref, b_ref, ws_ref, bs_ref, *rest):
    cast_src, y_ref, cast_dst = rest[:n_cast], rest[n_cast], rest[n_cast + 1:2 * n_cast + 1]
    (v_scr,) = rest[2 * n_cast + 1:]
    tm = x_ref.shape[0]
    xb = x_ref[...].astype(BF16)

    def gelu_proj(cs):
        z = jnp.dot(xb, win_ref[:, cs], preferred_element_type=F32) + bin_ref[:, cs]
        return _gelu(z)

    ncol = 512
    for c in range(A_WIDTH // ncol):
        v_scr[:, c * ncol:(c + 1) * ncol] = gelu_proj(slice(A_WIDTH + c * ncol, A_WIDTH + (c + 1) * ncol))
    groups = [slice(g * A_GROUP_DIM, (g + 1) * A_GROUP_DIM) for g in range(A_GROUPS)]
    u_next = gelu_proj(groups[0])
    v_scr[...] = _layer_norm(v_scr[...], g_ref[...], b_ref[...])
    row = lax.broadcasted_iota(jnp.int32, (CHUNK, CHUNK), 0)
    col = lax.broadcasted_iota(jnp.int32, (CHUNK, CHUNK), 1)
    causal = col <= row
    for g, gs in enumerate(groups):
        u = u_next
        if g + 1 < A_GROUPS:
            u_next = gelu_proj(groups[g + 1])
        wsg = jnp.where(causal, ws_ref[g], 0.0).astype(BF16)
        bsg = bs_ref[g]
        for c in range(tm // CHUNK):
            rs = slice(c * CHUNK, (c + 1) * CHUNK)
            s = jnp.dot(wsg, v_scr[rs, gs].astype(BF16), preferred_element_type=F32) + bsg
            y_ref[rs, gs] = (u[rs, :] * s).astype(BF16)
    _run_casts(cast_src, cast_dst)


def _sgu(x, w_in, b_in, g, b, w_s, b_s, layer, casts=()):
    S = x.shape[0]
    tm = TM_SGU
    nsteps = S // tm
    cast_specs = [_cast_specs(a, l, nsteps, lambda i: i) for a, l in casts]
    return pl.pallas_call(
        functools.partial(_sgu_kernel, len(casts)),
        out_shape=[jax.ShapeDtypeStruct((S, A_WIDTH), BF16)] + [c[2] for c in cast_specs],
        grid=(nsteps,),
        in_specs=[
            pl.BlockSpec((tm, D_MODEL), lambda i: (i, 0)),
            _resident(w_in.shape),
            _layer_slab((1, 2 * A_WIDTH), layer),
            _layer_slab((1, A_WIDTH), layer),
            _layer_slab((1, A_WIDTH), layer),
            _layer_slab((A_GROUPS, CHUNK, CHUNK), layer),
            _layer_slab((A_GROUPS, CHUNK, A_GROUP_DIM), layer),
        ] + [c[0] for c in cast_specs],
        out_specs=[pl.BlockSpec((tm, A_WIDTH), lambda i: (i, 0))] + [c[1] for c in cast_specs],
        scratch_shapes=[pltpu.VMEM((tm, A_WIDTH), F32)],
        compiler_params=_cparams("arbitrary"),
        name="sgu_mix",
    )(x, w_in, b_in, g, b, w_s, b_s, *[a for a, _ in casts])


def _proj_kernel(y_ref, w_ref, x_ref, g_ref, b_ref, o_ref):
    tm = x_ref.shape[0]
    chunks = [slice(r * ROWS_PROJ, (r + 1) * ROWS_PROJ) for r in range(tm // ROWS_PROJ)]
    for rs in chunks:
        o_ref[rs, :] = DEEPNORM_ALPHA * x_ref[rs, :] + jnp.dot(y_ref[rs, :], w_ref[...],
                                                               preferred_element_type=F32)
    for rs in chunks:
        o_ref[rs, :] = _layer_norm(o_ref[rs, :], g_ref[...], b_ref[...])


def _proj_res_ln(y, w, x, g, b, ln_row):
    S = x.shape[0]
    tm = TM_PROJ
    return pl.pallas_call(
        _proj_kernel,
        out_shape=jax.ShapeDtypeStruct((S, D_MODEL), F32),
        grid=(S // tm,),
        in_specs=[
            pl.BlockSpec((tm, y.shape[1]), lambda i: (i, 0)),
            _resident(w.shape),
            pl.BlockSpec((tm, D_MODEL), lambda i: (i, 0)),
            _layer_slab((1, D_MODEL), ln_row),
            _layer_slab((1, D_MODEL), ln_row),
        ],
        out_specs=pl.BlockSpec((tm, D_MODEL), lambda i: (i, 0)),
        compiler_params=_cparams("parallel"),
        name="proj_res_ln",
    )(y, w, x, g, b)


def _mlp_kernel(n_cast, x_ref, wu_ref, wd_ref, g_ref, b_ref, *rest):
    cast_src, o_ref, cast_dst = rest[:n_cast], rest[n_cast], rest[n_cast + 1:2 * n_cast + 1]
    xb_scr = rest[2 * n_cast + 1]
    j = pl.program_id(1)
    last = pl.num_programs(1) - 1

    def hidden():
        h = jnp.dot(xb_scr[...], wu_ref[...], preferred_element_type=F32)
        return jnp.square(jnp.maximum(h, 0.0)).astype(BF16)

    @pl.when(j == 0)
    def _():
        xb_scr[...] = x_ref[...].astype(BF16)
        o_ref[...] = DEEPNORM_ALPHA * x_ref[...] + jnp.dot(hidden(), wd_ref[...],
                                                           preferred_element_type=F32)
        _run_casts(cast_src, cast_dst)

    @pl.when(jnp.logical_and(j > 0, j < last))
    def _():
        o_ref[...] += jnp.dot(hidden(), wd_ref[...], preferred_element_type=F32)
        _run_casts(cast_src, cast_dst)

    @pl.when(j == last)
    def _():
        h = hidden()
        chunks = [slice(r * ROWS_LN, (r + 1) * ROWS_LN) for r in range(x_ref.shape[0] // ROWS_LN)]
        for rs in chunks:
            o_ref[rs, :] += jnp.dot(h[rs, :], wd_ref[...], preferred_element_type=F32)
        for rs in chunks:
            o_ref[rs, :] = _layer_norm(o_ref[rs, :], g_ref[...], b_ref[...])
        _run_casts(cast_src, cast_dst)


def _mlp(x, w_up, w_down, g, b, layer, casts=()):
    S = x.shape[0]
    tm, tf = TM_MLP, TF_MLP
    ni, nj = S // tm, D_FF // tf
    cast_specs = [_cast_specs(a, l, ni * nj, lambda i, j: i * nj + j) for a, l in casts]
    return pl.pallas_call(
        functools.partial(_mlp_kernel, len(casts)),
        out_shape=[jax.ShapeDtypeStruct((S, D_MODEL), F32)] + [c[2] for c in cast_specs],
        grid=(ni, nj),
        in_specs=[
            pl.BlockSpec((tm, D_MODEL), lambda i, j: (i, 0)),
            pl.BlockSpec((D_MODEL, tf), lambda i, j: (0, j)),
            pl.BlockSpec((tf, D_MODEL), lambda i, j: (j, 0)),
            _layer_slab((1, D_MODEL), 2 * layer + 1),
            _layer_slab((1, D_MODEL), 2 * layer + 1),
        ] + [c[0] for c in cast_specs],
        out_specs=[pl.BlockSpec((tm, D_MODEL), lambda i, j: (i, 0))] + [c[1] for c in cast_specs],
        scratch_shapes=[pltpu.VMEM((tm, D_MODEL), BF16)],
        compiler_params=_cparams("arbitrary", "arbitrary"),
        name="mlp_res_ln",
    )(x, w_up, w_down, g, b, *[a for a, _ in casts])


def _kv_kernel(x_ref, w_ref, b_ref, cos_ref, sin_ref, ek_ref, ev_ref, vone_ref, k_out, v_out):
    xb = x_ref[...].astype(BF16)
    kv = jnp.dot(xb, w_ref[...], preferred_element_type=F32) + b_ref[...]
    k1 = kv[:, :LANES]
    k2 = kv[:, LANES:2 * LANES]
    v = kv[:, KV_WIDTH:]
    c = cos_ref[...]
    s = sin_ref[...]
    kr = jnp.concatenate([k1 * c - k2 * s, k2 * c + k1 * s], axis=1).astype(BF16)
    kp = jnp.dot(kr, ek_ref[...], preferred_element_type=F32)
    vp = jnp.dot(v.astype(BF16), ev_ref[...], preferred_element_type=F32) + vone_ref[...]
    for h in range(N_KV_HEADS):
        for p in range(2):
            k0 = (h * 2 + p) * LANES
            k_out[h, p] = kp[:, k0:k0 + LANES].astype(BF16)
            v0 = (h * 2 + p) * 2 * LANES
            v_out[h, p] = vp[:, v0:v0 + 2 * LANES].astype(BF16)


def _kv_proj(x, w, b, cos_k, sin_k, ek, ev, vone):
    S = x.shape[0]
    tm = TM_KV
    return pl.pallas_call(
        _kv_kernel,
        out_shape=(jax.ShapeDtypeStruct((N_KV_HEADS, 2, S, LANES), BF16),
                   jax.ShapeDtypeStruct((N_KV_HEADS, 2, S, 2 * LANES), BF16)),
        grid=(S // tm,),
        in_specs=[
            pl.BlockSpec((tm, D_MODEL), lambda i: (i, 0)),
            _resident(w.shape),
            _resident(b.shape),
            pl.BlockSpec((tm, LANES), lambda i: (i, 0)),
            pl.BlockSpec((tm, LANES), lambda i: (i, 0)),
            _resident(ek.shape),
            _resident(ev.shape),
            _resident(vone.shape),
        ],
        out_specs=(pl.BlockSpec((N_KV_HEADS, 2, tm, LANES), lambda i: (0, 0, i, 0)),
                   pl.BlockSpec((N_KV_HEADS, 2, tm, 2 * LANES), lambda i: (0, 0, i, 0))),
        compiler_params=_cparams("parallel"),
        name="kv_proj",
    )(x, w, b, cos_k, sin_k, ek, ev, vone)


def _attn_kernel(sink_ref, x_ref, wq_ref, bq_ref, cos_ref, sin_ref, kp_ref, kc_ref, vp_ref,
                 vc_ref, bias_ref, eye_ref, o_ref, q_scr, k_scr, v_scr):
    tm = x_ref.shape[0]
    i = pl.program_id(0)
    xb = x_ref[...].astype(BF16)
    cos = cos_ref[...]
    sin = sin_ref[...]
    first_half = lax.broadcasted_iota(jnp.int32, (tm, LANES), 1) % HEAD_DIM < HALF
    ncol = 2 * LANES
    for c in range(D_MODEL // ncol):
        cs = slice(c * ncol, (c + 1) * ncol)
        q = jnp.dot(xb, wq_ref[:, cs], preferred_element_type=F32) + bq_ref[:, cs]
        for t in range(2):
            qt = q[:, t * LANES:(t + 1) * LANES]
            partner = jnp.where(first_half, pltpu.roll(qt, LANES - HALF, 1), pltpu.roll(qt, HALF, 1))
            qt = qt * cos + partner * sin
            q_scr[:, c * ncol + t * LANES:c * ncol + (t + 1) * LANES] = qt.astype(BF16)

    k_scr[:, :, :WINDOW, :] = kp_ref[...]
    k_scr[:, :, WINDOW:, :] = kc_ref[...]
    v_scr[:, :, :WINDOW, :] = vp_ref[...]
    v_scr[:, :, WINDOW:, :] = vc_ref[...]

    lane = lax.broadcasted_iota(jnp.int32, (WINDOW, LANES), 1)
    even_lanes = lane < HEAD_DIM
    first_tile = jnp.where(i == 0, 1, 0)
    nb = tm // WINDOW

    def stage_scores(b):
        rs = slice(b * WINDOW, (b + 1) * WINDOW)
        band = slice(b * WINDOW, b * WINDOW + BAND)
        mask_cols = bias_ref[first_tile] if b == 0 else bias_ref[0]
        scores = []
        for h in range(N_KV_HEADS):
            c0 = h * Q_PER_KV * HEAD_DIM
            lhs = jnp.concatenate(
                [q_scr[rs, c0 + j * LANES:c0 + (j + 1) * LANES] for j in range(PAIRS)], axis=0)
            kcat = jnp.concatenate([k_scr[h, 0, band, :], k_scr[h, 1, band, :]], axis=0)
            lhs = jnp.concatenate([lhs, eye_ref[...]], axis=1)
            kcat = jnp.concatenate([kcat, mask_cols], axis=1)
            scores.append(lax.dot_general(lhs, kcat, (((1,), (1,)), ((), ())),
                                          preferred_element_type=F32))
        return scores

    def stage_softmax(b, scores):
        pmats = []
        sinks = []
        for h in range(N_KV_HEADS):
            s = scores[h]
            p_rows = []
            sink_terms = []
            for j in range(PAIRS):
                p_cols = []
                sink_arg = []
                for p in range(2):
                    sub = s[j * WINDOW:(j + 1) * WINDOW, p * BAND:(p + 1) * BAND]
                    sk = sink_ref[h * Q_PER_KV + 2 * j + p]
                    m = jnp.maximum(jnp.max(sub, axis=1, keepdims=True), sk)
                    p_cols.append(jnp.exp2(sub - m).astype(BF16))
                    sink_arg.append(sk - m)
                p_rows.append(jnp.concatenate(p_cols, axis=1))
                sink_terms.append(jnp.exp2(jnp.where(even_lanes, sink_arg[0], sink_arg[1])))
            pmats.append(jnp.concatenate(p_rows, axis=0))
            sinks.append(sink_terms)
        return pmats, sinks

    def stage_values(b, pmats, sinks):
        rs = slice(b * WINDOW, (b + 1) * WINDOW)
        band = slice(b * WINDOW, b * WINDOW + BAND)
        for h in range(N_KV_HEADS):
            c0 = h * Q_PER_KV * HEAD_DIM
            vcat = jnp.concatenate([v_scr[h, 0, band, :], v_scr[h, 1, band, :]], axis=0)
            oa = jnp.dot(pmats[h], vcat, preferred_element_type=F32)
            for j in range(PAIRS):
                js = slice(j * WINDOW, (j + 1) * WINDOW)
                den = oa[js, LANES:] + sinks[h][j]
                o_ref[rs, c0 + j * LANES:c0 + (j + 1) * LANES] = (oa[js, :LANES] / den).astype(BF16)

    scores = stage_scores(0)
    for b in range(nb):
        nxt = stage_scores(b + 1) if b + 1 < nb else None
        pmats, sinks = stage_softmax(b, scores)
        stage_values(b, pmats, sinks)
        scores = nxt


def _attention(sinks, x, w_q, b_q, cos_q, sin_q, kmat, vmat, bias, eye, layer):
    S = x.shape[0]
    tm = TM_ATT
    nb = tm // WINDOW
    prev = lambda i, s: (0, 0, jnp.maximum(i * nb - 1, 0), 0)
    cur = lambda i, s: (0, 0, i, 0)
    grid_spec = pltpu.PrefetchScalarGridSpec(
        num_scalar_prefetch=1,
        grid=(S // tm,),
        in_specs=[
            pl.BlockSpec((tm, D_MODEL), lambda i, s: (i, 0)),
            _resident(w_q.shape),
            _layer_slab(b_q.shape[1:], layer),
            pl.BlockSpec((tm, LANES), lambda i, s: (i, 0)),
            pl.BlockSpec((tm, LANES), lambda i, s: (i, 0)),
            pl.BlockSpec((N_KV_HEADS, 2, WINDOW, LANES), prev),
            pl.BlockSpec((N_KV_HEADS, 2, tm, LANES), cur),
            pl.BlockSpec((N_KV_HEADS, 2, WINDOW, 2 * LANES), prev),
            pl.BlockSpec((N_KV_HEADS, 2, tm, 2 * LANES), cur),
            _resident(bias.shape),
            _resident(eye.shape),
        ],
        out_specs=pl.BlockSpec((tm, D_MODEL), lambda i, s: (i, 0)),
        scratch_shapes=[
            pltpu.VMEM((tm, D_MODEL), BF16),
            pltpu.VMEM((N_KV_HEADS, 2, tm + WINDOW, LANES), BF16),
            pltpu.VMEM((N_KV_HEADS, 2, tm + WINDOW, 2 * LANES), BF16),
        ],
    )
    return pl.pallas_call(
        _attn_kernel,
        out_shape=jax.ShapeDtypeStruct((S, D_MODEL), BF16),
        grid_spec=grid_spec,
        compiler_params=_cparams("parallel"),
        name="swa_attention",
    )(sinks, x, w_q, b_q, cos_q, sin_q, kmat, kmat, vmat, vmat, bias, eye)


def _k_perm():
    perm = np.zeros(KV_WIDTH, np.int32)
    for half in range(2):
        for h in range(N_KV_HEADS):
            for d in range(HALF):
                perm[half * LANES + h * HALF + d] = h * HEAD_DIM + half * HALF + d
    return perm


def _placements():
    ek = np.zeros((KV_WIDTH, N_KV_HEADS * 2 * LANES), np.float32)
    ev = np.zeros((KV_WIDTH, N_KV_HEADS * 2 * 2 * LANES), np.float32)
    vone = np.zeros((1, N_KV_HEADS * 2 * 2 * LANES), np.float32)
    for h in range(N_KV_HEADS):
        for p in range(2):
            for half in range(2):
                for d in range(HALF):
                    ek[half * LANES + h * HALF + d,
                       (h * 2 + p) * LANES + p * HEAD_DIM + half * HALF + d] = 1.0
            for d in range(HEAD_DIM):
                base = (h * 2 + p) * 2 * LANES
                ev[h * HEAD_DIM + d, base + p * HEAD_DIM + d] = 1.0
                vone[0, base + LANES + p * HEAD_DIM + d] = 1.0
    return ek, ev, vone


def _band_mask():
    i = np.arange(WINDOW)[None, :]
    j = np.arange(BAND)[:, None]
    in_band = (j > i) & (j <= i + WINDOW)
    mask = np.zeros((2, BAND, WINDOW), np.float32)
    mask[0] = np.where(in_band, 0.0, NEG_BF16)
    mask[1] = np.where(in_band & (j >= WINDOW), 0.0, NEG_BF16)
    eye = np.tile(np.eye(WINDOW, dtype=np.float32), (PAIRS, 1))
    return np.tile(mask, (1, 2, 1)), eye


def kernel(x, a_w_in, a_b_in, a_ln_v_g, a_ln_v_b, a_w_s, a_b_s, a_w_out, kv_w, kv_b, b_w_q,
           b_b_q, b_sinks, b_w_o, mlp_w_up, mlp_w_down, ln_g, ln_b):
    B, S, _ = x.shape
    assert B == 1 and S % TM_MLP == 0
    h = x.reshape(S, D_MODEL)

    inv_freq = jnp.tile(ROPE_THETA ** (-jnp.arange(0, HEAD_DIM, 2, dtype=F32) / HEAD_DIM), 4)
    blk = (jnp.arange(S // WINDOW, dtype=jnp.int32) * WINDOW).astype(F32)[:, None] * inv_freq[None, :]
    off = jnp.arange(WINDOW, dtype=jnp.int32).astype(F32)[:, None] * inv_freq[None, :]
    cb, sb = jnp.cos(blk)[:, None, :], jnp.sin(blk)[:, None, :]
    co, so = jnp.cos(off)[None, :, :], jnp.sin(off)[None, :, :]
    cos_k = (cb * co - sb * so).reshape(S, LANES)
    sin_k = (sb * co + cb * so).reshape(S, LANES)
    scale = HEAD_DIM ** -0.5 * LOG2_E
    rot_sign = np.tile(np.repeat(np.array([-scale, scale], np.float32), HALF), 2)
    cos_q = cos_k * scale
    sin_q = sin_k * rot_sign[None, :]

    k_perm = _k_perm()
    ek, ev, vone = _placements()
    ek = jnp.asarray(ek, BF16)
    ev = jnp.asarray(ev, BF16)
    vone = jnp.asarray(vone, F32)
    mask, eye = _band_mask()
    mask = jnp.asarray(mask, BF16)
    eye = jnp.asarray(eye, BF16)

    rows = lambda v: v.reshape(-1, 1, v.shape[-1])
    b_in, ln_v_g, ln_v_b = rows(a_b_in), rows(a_ln_v_g), rows(a_ln_v_b)
    b_q = rows(b_b_q)
    g_rows, b_rows = rows(ln_g), rows(ln_b)
    bs = jnp.broadcast_to(a_b_s[:, :, :, None], (N_A_LAYERS, A_GROUPS, CHUNK, A_GROUP_DIM))

    def mixer_casts(layer):
        if layer < N_A_LAYERS:
            return [(a_w_in, layer), (a_w_out, layer)]
        return [(b_w_q, layer - N_A_LAYERS), (b_w_o, layer - N_A_LAYERS)]

    def layer_casts(layer):
        return mixer_casts(layer) + [(mlp_w_up, layer), (mlp_w_down, layer)]

    w_a = a_w_in[0].astype(BF16)
    for layer in range(DEPTH):
        nxt = layer_casts(layer + 1) if layer + 1 < DEPTH else []
        if layer < N_A_LAYERS:
            first = [(a_w_out, 0), (mlp_w_up, 0), (mlp_w_down, 0)] if layer == 0 else []
            y, *done = _sgu(h, w_a, b_in, ln_v_g, ln_v_b, a_w_s, bs, layer, first)
            if layer == 0:
                w_b, w_up, w_down = done
        else:
            if layer == N_A_LAYERS:
                w_kv = jnp.concatenate([kv_w[:, :KV_WIDTH][:, k_perm], kv_w[:, KV_WIDTH:]], axis=1)
                b_kv = jnp.concatenate([kv_b[:KV_WIDTH][k_perm], kv_b[KV_WIDTH:]])
                kmat, vmat = _kv_proj(h, w_kv.astype(BF16), b_kv.reshape(1, -1), cos_k, sin_k,
                                      ek, ev, vone)
            j = layer - N_A_LAYERS
            y = _attention(b_sinks[j] * LOG2_E, h, w_a, b_q, cos_q, sin_q, kmat, vmat, mask, eye, j)
        h = _proj_res_ln(y, w_b, h, g_rows, b_rows, 2 * layer)
        h, *done = _mlp(h, w_up, w_down, g_rows, b_rows, layer, nxt)
        if nxt:
            w_a, w_b, w_up, w_down = done
    return h.reshape(B, S, D_MODEL)
```

```python
import functools

import numpy as np
import jax
import jax.numpy as jnp
from jax import lax
from jax.experimental import pallas as pl
from jax.experimental.pallas import tpu as pltpu

D_MODEL = 2048
DEPTH = 4
N_A_LAYERS = DEPTH // 2
CHUNK = 128
A_WIDTH = D_MODEL
A_GROUPS = 8
A_GROUP_DIM = A_WIDTH // A_GROUPS
HEAD_DIM = 64
HALF = HEAD_DIM // 2
N_Q_HEADS = D_MODEL // HEAD_DIM
N_KV_HEADS = 4
Q_PER_KV = N_Q_HEADS // N_KV_HEADS
PAIRS = Q_PER_KV // 2
WINDOW = 128
BAND = 2 * WINDOW
ROPE_THETA = 10000.0
D_FF = 4 * D_MODEL
LN_EPS = 1e-5
DEEPNORM_ALPHA = (2.0 * DEPTH) ** 0.25
KV_WIDTH = N_KV_HEADS * HEAD_DIM

LANES = 128
V7X_VMEM_LIMIT = 56 * 1024 * 1024

BF16 = jnp.bfloat16
F32 = jnp.float32
NEG_BF16 = float(jnp.finfo(jnp.bfloat16).min)
LOG2_E = float(np.log2(np.e))

TM_SGU = 512
TM_PROJ = 1024
TM_MLP = 1024
TF_MLP = 512
TM_KV = 1024
TM_ATT = 512
ROWS_LN = 256
ROWS_PROJ = 256


def _cparams(*sem):
    return pltpu.CompilerParams(dimension_semantics=sem, vmem_limit_bytes=V7X_VMEM_LIMIT)


def _resident(shape):
    nd = len(shape)
    return pl.BlockSpec(shape, lambda *_: (0,) * nd, pipeline_mode=pl.Buffered(1))


def _layer_slab(tail, layer):
    nd = len(tail)
    return pl.BlockSpec((None,) + tuple(tail), lambda *_: (layer,) + (0,) * nd,
                        pipeline_mode=pl.Buffered(1))


BF16_SUBLANES = 16


def _cast_specs(stacked, layer, nsteps, step_of):
    _, nrows, ncols = stacked.shape
    nblk = nsteps
    while nrows % nblk or (nrows // nblk) % BF16_SUBLANES:
        nblk //= 2
    rb, group = nrows // nblk, nsteps // nblk
    in_spec = pl.BlockSpec((None, rb, ncols), lambda *ids: (layer, step_of(*ids) // group, 0))
    out_spec = pl.BlockSpec((rb, ncols), lambda *ids: (step_of(*ids) // group, 0))
    return in_spec, out_spec, jax.ShapeDtypeStruct((nrows, ncols), BF16)


def _run_casts(srcs, dsts):
    for src, dst in zip(srcs, dsts):
        dst[...] = src[...].astype(BF16)


def _gelu(t):
    return 0.5 * t * (1.0 + lax.erf(t * np.float32(np.sqrt(0.5))))


def _layer_norm(t, g, b):
    mu = jnp.mean(t, axis=-1, keepdims=True)
    c = t - mu
    var = jnp.mean(c * c, axis=-1, keepdims=True)
    return c * lax.rsqrt(var + LN_EPS) * g + b


def _sgu_kernel(n_cast, x_ref, win_ref, bin_ref, g_ref, b_ref, ws_ref, bs_ref, *rest):
    cast_src, y_ref, cast_dst = rest[:n_cast], rest[n_cast], rest[n_cast + 1:2 * n_cast + 1]
    (v_scr,) = rest[2 * n_cast + 1:]
    tm = x_ref.shape[0]
    xb = x_ref[...].astype(BF16)

    def gelu_proj(cs):
        z = jnp.dot(xb, win_ref[:, cs], preferred_element_type=F32) + bin_ref[:, cs]
        return _gelu(z)

    ncol = 512
    for c in range(A_WIDTH // ncol):
        v_scr[:, c * ncol:(c + 1) * ncol] = gelu_proj(slice(A_WIDTH + c * ncol, A_WIDTH + (c + 1) * ncol))
    groups = [slice(g * A_GROUP_DIM, (g + 1) * A_GROUP_DIM) for g in range(A_GROUPS)]
    ahead = 2
    u_queue = [gelu_proj(groups[g]) for g in range(ahead)]
    v_scr[...] = _layer_norm(v_scr[...], g_ref[...], b_ref[...])
    row = lax.broadcasted_iota(jnp.int32, (CHUNK, CHUNK), 0)
    col = lax.broadcasted_iota(jnp.int32, (CHUNK, CHUNK), 1)
    causal = col <= row
    for g, gs in enumerate(groups):
        u = u_queue.pop(0)
        if g + ahead < A_GROUPS:
            u_queue.append(gelu_proj(groups[g + ahead]))
        wsg = jnp.where(causal, ws_ref[g], 0.0).astype(BF16)
        bsg = bs_ref[g]
        for c in range(tm // CHUNK):
            rs = slice(c * CHUNK, (c + 1) * CHUNK)
            s = jnp.dot(wsg, v_scr[rs, gs].astype(BF16), preferred_element_type=F32) + bsg
            y_ref[rs, gs] = (u[rs, :] * s).astype(BF16)
    _run_casts(cast_src, cast_dst)


def _sgu(x, w_in, b_in, g, b, w_s, b_s, layer, casts=()):
    S = x.shape[0]
    tm = TM_SGU
    nsteps = S // tm
    cast_specs = [_cast_specs(a, l, nsteps, lambda i: i) for a, l in casts]
    return pl.pallas_call(
        functools.partial(_sgu_kernel, len(casts)),
        out_shape=[jax.ShapeDtypeStruct((S, A_WIDTH), BF16)] + [c[2] for c in cast_specs],
        grid=(nsteps,),
        in_specs=[
            pl.BlockSpec((tm, D_MODEL), lambda i: (i, 0)),
            _resident(w_in.shape),
            _layer_slab((1, 2 * A_WIDTH), layer),
            _layer_slab((1, A_WIDTH), layer),
            _layer_slab((1, A_WIDTH), layer),
            _layer_slab((A_GROUPS, CHUNK, CHUNK), layer),
            _layer_slab((A_GROUPS, CHUNK, A_GROUP_DIM), layer),
        ] + [c[0] for c in cast_specs],
        out_specs=[pl.BlockSpec((tm, A_WIDTH), lambda i: (i, 0))] + [c[1] for c in cast_specs],
        scratch_shapes=[pltpu.VMEM((tm, A_WIDTH), F32)],
        compiler_params=_cparams("arbitrary"),
        name="sgu_mix",
    )(x, w_in, b_in, g, b, w_s, b_s, *[a for a, _ in casts])


def _proj_kernel(y_ref, w_ref, x_ref, g_ref, b_ref, o_ref):
    tm = x_ref.shape[0]
    chunks = [slice(r * ROWS_PROJ, (r + 1) * ROWS_PROJ) for r in range(tm // ROWS_PROJ)]
    for rs in chunks:
        o_ref[rs, :] = DEEPNORM_ALPHA * x_ref[rs, :] + jnp.dot(y_ref[rs, :], w_ref[...],
                                                               preferred_element_type=F32)
    for rs in chunks:
        o_ref[rs, :] = _layer_norm(o_ref[rs, :], g_ref[...], b_ref[...])


def _proj_res_ln(y, w, x, g, b, ln_row):
    S = x.shape[0]
    tm = TM_PROJ
    return pl.pallas_call(
        _proj_kernel,
        out_shape=jax.ShapeDtypeStruct((S, D_MODEL), F32),
        grid=(S // tm,),
        in_specs=[
            pl.BlockSpec((tm, y.shape[1]), lambda i: (i, 0)),
            _resident(w.shape),
            pl.BlockSpec((tm, D_MODEL), lambda i: (i, 0)),
            _layer_slab((1, D_MODEL), ln_row),
            _layer_slab((1, D_MODEL), ln_row),
        ],
        out_specs=pl.BlockSpec((tm, D_MODEL), lambda i: (i, 0)),
        compiler_params=_cparams("parallel"),
        name="proj_res_ln",
    )(y, w, x, g, b)


def _mlp_kernel(n_cast, x_ref, wu_ref, wd_ref, g_ref, b_ref, *rest):
    cast_src, o_ref, cast_dst = rest[:n_cast], rest[n_cast], rest[n_cast + 1:2 * n_cast + 1]
    xb_scr = rest[2 * n_cast + 1]
    j = pl.program_id(1)
    last = pl.num_programs(1) - 1

    def hidden():
        h = jnp.dot(xb_scr[...], wu_ref[...], preferred_element_type=F32)
        return jnp.square(jnp.maximum(h, 0.0)).astype(BF16)

    @pl.when(j == 0)
    def _():
        xb_scr[...] = x_ref[...].astype(BF16)
        o_ref[...] = DEEPNORM_ALPHA * x_ref[...] + jnp.dot(hidden(), wd_ref[...],
                                                           preferred_element_type=F32)
        _run_casts(cast_src, cast_dst)

    @pl.when(jnp.logical_and(j > 0, j < last))
    def _():
        o_ref[...] += jnp.dot(hidden(), wd_ref[...], preferred_element_type=F32)
        _run_casts(cast_src, cast_dst)

    @pl.when(j == last)
    def _():
        h = hidden()
        chunks = [slice(r * ROWS_LN, (r + 1) * ROWS_LN) for r in range(x_ref.shape[0] // ROWS_LN)]
        for rs in chunks:
            o_ref[rs, :] += jnp.dot(h[rs, :], wd_ref[...], preferred_element_type=F32)
        for rs in chunks:
            o_ref[rs, :] = _layer_norm(o_ref[rs, :], g_ref[...], b_ref[...])
        _run_casts(cast_src, cast_dst)


def _mlp(x, w_up, w_down, g, b, layer, casts=()):
    S = x.shape[0]
    tm, tf = TM_MLP, TF_MLP
    ni, nj = S // tm, D_FF // tf
    cast_specs = [_cast_specs(a, l, ni * nj, lambda i, j: i * nj + j) for a, l in casts]
    return pl.pallas_call(
        functools.partial(_mlp_kernel, len(casts)),
        out_shape=[jax.ShapeDtypeStruct((S, D_MODEL), F32)] + [c[2] for c in cast_specs],
        grid=(ni, nj),
        in_specs=[
            pl.BlockSpec((tm, D_MODEL), lambda i, j: (i, 0)),
            pl.BlockSpec((D_MODEL, tf), lambda i, j: (0, j)),
            pl.BlockSpec((tf, D_MODEL), lambda i, j: (j, 0)),
            _layer_slab((1, D_MODEL), 2 * layer + 1),
            _layer_slab((1, D_MODEL), 2 * layer + 1),
        ] + [c[0] for c in cast_specs],
        out_specs=[pl.BlockSpec((tm, D_MODEL), lambda i, j: (i, 0))] + [c[1] for c in cast_specs],
        scratch_shapes=[pltpu.VMEM((tm, D_MODEL), BF16)],
        compiler_params=_cparams("arbitrary", "arbitrary"),
        name="mlp_res_ln",
    )(x, w_up, w_down, g, b, *[a for a, _ in casts])


def _kv_kernel(x_ref, w_ref, b_ref, cos_ref, sin_ref, ek_ref, ev_ref, vone_ref, k_out, v_out):
    xb = x_ref[...].astype(BF16)
    kv = jnp.dot(xb, w_ref[...], preferred_element_type=F32) + b_ref[...]
    k1 = kv[:, :LANES]
    k2 = kv[:, LANES:2 * LANES]
    v = kv[:, KV_WIDTH:]
    c = cos_ref[...]
    s = sin_ref[...]
    kr = jnp.concatenate([k1 * c - k2 * s, k2 * c + k1 * s], axis=1).astype(BF16)
    kp = jnp.dot(kr, ek_ref[...], preferred_element_type=F32)
    vp = jnp.dot(v.astype(BF16), ev_ref[...], preferred_element_type=F32) + vone_ref[...]
    for h in range(N_KV_HEADS):
        for p in range(2):
            k0 = (h * 2 + p) * LANES
            k_out[h, p] = kp[:, k0:k0 + LANES].astype(BF16)
            v0 = (h * 2 + p) * 2 * LANES
            v_out[h, p] = vp[:, v0:v0 + 2 * LANES].astype(BF16)


def _kv_proj(x, w, b, cos_k, sin_k, ek, ev, vone):
    S = x.shape[0]
    tm = TM_KV
    return pl.pallas_call(
        _kv_kernel,
        out_shape=(jax.ShapeDtypeStruct((N_KV_HEADS, 2, S, LANES), BF16),
                   jax.ShapeDtypeStruct((N_KV_HEADS, 2, S, 2 * LANES), BF16)),
        grid=(S // tm,),
        in_specs=[
            pl.BlockSpec((tm, D_MODEL), lambda i: (i, 0)),
            _resident(w.shape),
            _resident(b.shape),
            pl.BlockSpec((tm, LANES), lambda i: (i, 0)),
            pl.BlockSpec((tm, LANES), lambda i: (i, 0)),
            _resident(ek.shape),
            _resident(ev.shape),
            _resident(vone.shape),
        ],
        out_specs=(pl.BlockSpec((N_KV_HEADS, 2, tm, LANES), lambda i: (0, 0, i, 0)),
                   pl.BlockSpec((N_KV_HEADS, 2, tm, 2 * LANES), lambda i: (0, 0, i, 0))),
        compiler_params=_cparams("parallel"),
        name="kv_proj",
    )(x, w, b, cos_k, sin_k, ek, ev, vone)


def _attn_kernel(sink_ref, x_ref, wq_ref, bq_ref, cos_ref, sin_ref, kp_ref, kc_ref, vp_ref,
                 vc_ref, bias_ref, eye_ref, o_ref, q_scr, k_scr, v_scr):
    tm = x_ref.shape[0]
    i = pl.program_id(0)
    xb = x_ref[...].astype(BF16)
    cos = cos_ref[...]
    sin = sin_ref[...]
    first_half = lax.broadcasted_iota(jnp.int32, (tm, LANES), 1) % HEAD_DIM < HALF
    ncol = 2 * LANES
    for c in range(D_MODEL // ncol):
        cs = slice(c * ncol, (c + 1) * ncol)
        q = jnp.dot(xb, wq_ref[:, cs], preferred_element_type=F32) + bq_ref[:, cs]
        for t in range(2):
            qt = q[:, t * LANES:(t + 1) * LANES]
            partner = jnp.where(first_half, pltpu.roll(qt, LANES - HALF, 1), pltpu.roll(qt, HALF, 1))
            qt = qt * cos + partner * sin
            q_scr[:, c * ncol + t * LANES:c * ncol + (t + 1) * LANES] = qt.astype(BF16)

    k_scr[:, :, :WINDOW, :] = kp_ref[...]
    k_scr[:, :, WINDOW:, :] = kc_ref[...]
    v_scr[:, :, :WINDOW, :] = vp_ref[...]
    v_scr[:, :, WINDOW:, :] = vc_ref[...]

    lane = lax.broadcasted_iota(jnp.int32, (WINDOW, LANES), 1)
    even_lanes = lane < HEAD_DIM
    first_tile = jnp.where(i == 0, 1, 0)
    nb = tm // WINDOW

    def stage_scores(b):
        rs = slice(b * WINDOW, (b + 1) * WINDOW)
        band = slice(b * WINDOW, b * WINDOW + BAND)
        mask_cols = bias_ref[first_tile] if b == 0 else bias_ref[0]
        scores = []
        for h in range(N_KV_HEADS):
            c0 = h * Q_PER_KV * HEAD_DIM
            lhs = jnp.concatenate(
                [q_scr[rs, c0 + j * LANES:c0 + (j + 1) * LANES] for j in range(PAIRS)], axis=0)
            kcat = jnp.concatenate([k_scr[h, 0, band, :], k_scr[h, 1, band, :]], axis=0)
            lhs = jnp.concatenate([lhs, eye_ref[...]], axis=1)
            kcat = jnp.concatenate([kcat, mask_cols], axis=1)
            scores.append(lax.dot_general(lhs, kcat, (((1,), (1,)), ((), ())),
                                          preferred_element_type=F32))
        return scores

    def stage_softmax(b, scores):
        pmats = []
        sinks = []
        for h in range(N_KV_HEADS):
            s = scores[h]
            p_rows = []
            sink_terms = []
            for j in range(PAIRS):
                p_cols = []
                sink_arg = []
                for p in range(2):
                    sub = s[j * WINDOW:(j + 1) * WINDOW, p * BAND:(p + 1) * BAND]
                    sk = sink_ref[h * Q_PER_KV + 2 * j + p]
                    m = jnp.maximum(jnp.max(sub, axis=1, keepdims=True), sk)
                    p_cols.append(jnp.exp2(sub - m).astype(BF16))
                    sink_arg.append(sk - m)
                p_rows.append(jnp.concatenate(p_cols, axis=1))
                sink_terms.append(jnp.exp2(jnp.where(even_lanes, sink_arg[0], sink_arg[1])))
            pmats.append(jnp.concatenate(p_rows, axis=0))
            sinks.append(sink_terms)
        return pmats, sinks

    def stage_values(b, pmats, sinks):
        rs = slice(b * WINDOW, (b + 1) * WINDOW)
        band = slice(b * WINDOW, b * WINDOW + BAND)
        for h in range(N_KV_HEADS):
            c0 = h * Q_PER_KV * HEAD_DIM
            vcat = jnp.concatenate([v_scr[h, 0, band, :], v_scr[h, 1, band, :]], axis=0)
            oa = jnp.dot(pmats[h], vcat, preferred_element_type=F32)
            for j in range(PAIRS):
                js = slice(j * WINDOW, (j + 1) * WINDOW)
                den = oa[js, LANES:] + sinks[h][j]
                o_ref[rs, c0 + j * LANES:c0 + (j + 1) * LANES] = (oa[js, :LANES] / den).astype(BF16)

    scores = stage_scores(0)
    for b in range(nb):
        nxt = stage_scores(b + 1) if b + 1 < nb else None
        pmats, sinks = stage_softmax(b, scores)
        stage_values(b, pmats, sinks)
        scores = nxt


def _attention(sinks, x, w_q, b_q, cos_q, sin_q, kmat, vmat, bias, eye, layer):
    S = x.shape[0]
    tm = TM_ATT
    nb = tm // WINDOW
    prev = lambda i, s: (0, 0, jnp.maximum(i * nb - 1, 0), 0)
    cur = lambda i, s: (0, 0, i, 0)
    grid_spec = pltpu.PrefetchScalarGridSpec(
        num_scalar_prefetch=1,
        grid=(S // tm,),
        in_specs=[
            pl.BlockSpec((tm, D_MODEL), lambda i, s: (i, 0)),
            _resident(w_q.shape),
            _layer_slab(b_q.shape[1:], layer),
            pl.BlockSpec((tm, LANES), lambda i, s: (i, 0)),
            pl.BlockSpec((tm, LANES), lambda i, s: (i, 0)),
            pl.BlockSpec((N_KV_HEADS, 2, WINDOW, LANES), prev),
            pl.BlockSpec((N_KV_HEADS, 2, tm, LANES), cur),
            pl.BlockSpec((N_KV_HEADS, 2, WINDOW, 2 * LANES), prev),
            pl.BlockSpec((N_KV_HEADS, 2, tm, 2 * LANES), cur),
            _resident(bias.shape),
            _resident(eye.shape),
        ],
        out_specs=pl.BlockSpec((tm, D_MODEL), lambda i, s: (i, 0)),
        scratch_shapes=[
            pltpu.VMEM((tm, D_MODEL), BF16),
            pltpu.VMEM((N_KV_HEADS, 2, tm + WINDOW, LANES), BF16),
            pltpu.VMEM((N_KV_HEADS, 2, tm + WINDOW, 2 * LANES), BF16),
        ],
    )
    return pl.pallas_call(
        _attn_kernel,
        out_shape=jax.ShapeDtypeStruct((S, D_MODEL), BF16),
        grid_spec=grid_spec,
        compiler_params=_cparams("parallel"),
        name="swa_attention",
    )(sinks, x, w_q, b_q, cos_q, sin_q, kmat, kmat, vmat, vmat, bias, eye)


def _k_perm():
    perm = np.zeros(KV_WIDTH, np.int32)
    for half in range(2):
        for h in range(N_KV_HEADS):
            for d in range(HALF):
                perm[half * LANES + h * HALF + d] = h * HEAD_DIM + half * HALF + d
    return perm


def _placements():
    ek = np.zeros((KV_WIDTH, N_KV_HEADS * 2 * LANES), np.float32)
    ev = np.zeros((KV_WIDTH, N_KV_HEADS * 2 * 2 * LANES), np.float32)
    vone = np.zeros((1, N_KV_HEADS * 2 * 2 * LANES), np.float32)
    for h in range(N_KV_HEADS):
        for p in range(2):
            for half in range(2):
                for d in range(HALF):
                    ek[half * LANES + h * HALF + d,
                       (h * 2 + p) * LANES + p * HEAD_DIM + half * HALF + d] = 1.0
            for d in range(HEAD_DIM):
                base = (h * 2 + p) * 2 * LANES
                ev[h * HEAD_DIM + d, base + p * HEAD_DIM + d] = 1.0
                vone[0, base + LANES + p * HEAD_DIM + d] = 1.0
    return ek, ev, vone


def _band_mask():
    i = np.arange(WINDOW)[None, :]
    j = np.arange(BAND)[:, None]
    in_band = (j > i) & (j <= i + WINDOW)
    mask = np.zeros((2, BAND, WINDOW), np.float32)
    mask[0] = np.where(in_band, 0.0, NEG_BF16)
    mask[1] = np.where(in_band & (j >= WINDOW), 0.0, NEG_BF16)
    eye = np.tile(np.eye(WINDOW, dtype=np.float32), (PAIRS, 1))
    return np.tile(mask, (1, 2, 1)), eye


def kernel(x, a_w_in, a_b_in, a_ln_v_g, a_ln_v_b, a_w_s, a_b_s, a_w_out, kv_w, kv_b, b_w_q,
           b_b_q, b_sinks, b_w_o, mlp_w_up, mlp_w_down, ln_g, ln_b):
    B, S, _ = x.shape
    assert B == 1 and S % TM_MLP == 0
    h = x.reshape(S, D_MODEL)

    inv_freq = jnp.tile(ROPE_THETA ** (-jnp.arange(0, HEAD_DIM, 2, dtype=F32) / HEAD_DIM), 4)
    blk = (jnp.arange(S // WINDOW, dtype=jnp.int32) * WINDOW).astype(F32)[:, None] * inv_freq[None, :]
    off = jnp.arange(WINDOW, dtype=jnp.int32).astype(F32)[:, None] * inv_freq[None, :]
    cb, sb = jnp.cos(blk)[:, None, :], jnp.sin(blk)[:, None, :]
    co, so = jnp.cos(off)[None, :, :], jnp.sin(off)[None, :, :]
    cos_k = (cb * co - sb * so).reshape(S, LANES)
    sin_k = (sb * co + cb * so).reshape(S, LANES)
    scale = HEAD_DIM ** -0.5 * LOG2_E
    rot_sign = np.tile(np.repeat(np.array([-scale, scale], np.float32), HALF), 2)
    cos_q = cos_k * scale
    sin_q = sin_k * rot_sign[None, :]

    k_perm = _k_perm()
    ek, ev, vone = _placements()
    ek = jnp.asarray(ek, BF16)
    ev = jnp.asarray(ev, BF16)
    vone = jnp.asarray(vone, F32)
    mask, eye = _band_mask()
    mask = jnp.asarray(mask, BF16)
    eye = jnp.asarray(eye, BF16)

    rows = lambda v: v.reshape(-1, 1, v.shape[-1])
    b_in, ln_v_g, ln_v_b = rows(a_b_in), rows(a_ln_v_g), rows(a_ln_v_b)
    b_q = rows(b_b_q)
    g_rows, b_rows = rows(ln_g), rows(ln_b)
    bs = jnp.broadcast_to(a_b_s[:, :, :, None], (N_A_LAYERS, A_GROUPS, CHUNK, A_GROUP_DIM))

    def mixer_casts(layer):
        if layer < N_A_LAYERS:
            return [(a_w_in, layer), (a_w_out, layer)]
        return [(b_w_q, layer - N_A_LAYERS), (b_w_o, layer - N_A_LAYERS)]

    def layer_casts(layer):
        return mixer_casts(layer) + [(mlp_w_up, layer), (mlp_w_down, layer)]

    w_a = a_w_in[0].astype(BF16)
    for layer in range(DEPTH):
        nxt = layer_casts(layer + 1) if layer + 1 < DEPTH else []
        if layer < N_A_LAYERS:
            first = [(a_w_out, 0), (mlp_w_up, 0), (mlp_w_down, 0)] if layer == 0 else []
            y, *done = _sgu(h, w_a, b_in, ln_v_g, ln_v_b, a_w_s, bs, layer, first)
            if layer == 0:
                w_b, w_up, w_down = done
        else:
            if layer == N_A_LAYERS:
                w_kv = jnp.concatenate([kv_w[:, :KV_WIDTH][:, k_perm], kv_w[:, KV_WIDTH:]], axis=1)
                b_kv = jnp.concatenate([kv_b[:KV_WIDTH][k_perm], kv_b[KV_WIDTH:]])
                kmat, vmat = _kv_proj(h, w_kv.astype(BF16), b_kv.reshape(1, -1), cos_k, sin_k,
                                      ek, ev, vone)
            j = layer - N_A_LAYERS
            y = _attention(b_sinks[j] * LOG2_E, h, w_a, b_q, cos_q, sin_q, kmat, vmat, mask, eye, j)
        h = _proj_res_ln(y, w_b, h, g_rows, b_rows, 2 * layer)
        h, *done = _mlp(h, w_up, w_down, g_rows, b_rows, layer, nxt)
        if nxt:
            w_a, w_b, w_up, w_down = done
    return h.reshape(B, S, D_MODEL)
```

```python
import functools

import numpy as np
import jax
import jax.numpy as jnp
from jax import lax
from jax.experimental import pallas as pl
from jax.experimental.pallas import tpu as pltpu

D_MODEL = 2048
DEPTH = 4
N_A_LAYERS = DEPTH // 2
CHUNK = 128
A_WIDTH = D_MODEL
A_GROUPS = 8
A_GROUP_DIM = A_WIDTH // A_GROUPS
HEAD_DIM = 64
HALF = HEAD_DIM // 2
N_Q_HEADS = D_MODEL // HEAD_DIM
N_KV_HEADS = 4
Q_PER_KV = N_Q_HEADS // N_KV_HEADS
PAIRS = Q_PER_KV // 2
WINDOW = 128
BAND = 2 * WINDOW
ROPE_THETA = 10000.0
D_FF = 4 * D_MODEL
LN_EPS = 1e-5
DEEPNORM_ALPHA = (2.0 * DEPTH) ** 0.25
KV_WIDTH = N_KV_HEADS * HEAD_DIM

LANES = 128
V7X_VMEM_LIMIT = 56 * 1024 * 1024

BF16 = jnp.bfloat16
F32 = jnp.float32
NEG_BF16 = float(jnp.finfo(jnp.bfloat16).min)
LOG2_E = float(np.log2(np.e))

TM_SGU = 512
TM_PROJ = 1024
TM_MLP = 1024
TF_MLP = 512
TM_KV = 1024
TM_ATT = 512
ROWS_LN = 256
ROWS_PROJ = 256


def _cparams(*sem):
    return pltpu.CompilerParams(dimension_semantics=sem, vmem_limit_bytes=V7X_VMEM_LIMIT)


def _resident(shape):
    nd = len(shape)
    return pl.BlockSpec(shape, lambda *_: (0,) * nd, pipeline_mode=pl.Buffered(1))


def _layer_slab(tail, layer):
    nd = len(tail)
    return pl.BlockSpec((None,) + tuple(tail), lambda *_: (layer,) + (0,) * nd,
                        pipeline_mode=pl.Buffered(1))


BF16_SUBLANES = 16


def _cast_specs(stacked, layer, nsteps, step_of):
    _, nrows, ncols = stacked.shape
    nblk = nsteps
    while nrows % nblk or (nrows // nblk) % BF16_SUBLANES:
        nblk //= 2
    rb, group = nrows // nblk, nsteps // nblk
    in_spec = pl.BlockSpec((None, rb, ncols), lambda *ids: (layer, step_of(*ids) // group, 0))
    out_spec = pl.BlockSpec((rb, ncols), lambda *ids: (step_of(*ids) // group, 0))
    return in_spec, out_spec, jax.ShapeDtypeStruct((nrows, ncols), BF16)


def _run_casts(srcs, dsts):
    for src, dst in zip(srcs, dsts):
        dst[...] = src[...].astype(BF16)


def _gelu(t):
    return 0.5 * t * (1.0 + lax.erf(t * np.float32(np.sqrt(0.5))))


def _layer_norm(t, g, b):
    mu = jnp.mean(t, axis=-1, keepdims=True)
    c = t - mu
    var = jnp.mean(c * c, axis=-1, keepdims=True)
    return c * lax.rsqrt(var + LN_EPS) * g + b


def _sgu_kernel(n_cast, x_ref, win_ref, bin_ref, g_ref, b_ref, ws_ref, bs_ref, *rest):
    cast_src, y_ref, cast_dst = rest[:n_cast], rest[n_cast], rest[n_cast + 1:2 * n_cast + 1]
    (v_scr,) = rest[2 * n_cast + 1:]
    tm = x_ref.shape[0]
    xb = x_ref[...].astype(BF16)

    def gelu_proj(cs):
        z = jnp.dot(xb, win_ref[:, cs], preferred_element_type=F32) + bin_ref[:, cs]
        return _gelu(z)

    ncol = 512
    for c in range(A_WIDTH // ncol):
        v_scr[:, c * ncol:(c + 1) * ncol] = gelu_proj(slice(A_WIDTH + c * ncol, A_WIDTH + (c + 1) * ncol))
    groups = [slice(g * A_GROUP_DIM, (g + 1) * A_GROUP_DIM) for g in range(A_GROUPS)]
    ahead = 2
    u_queue = [gelu_proj(groups[g]) for g in range(ahead)]
    v_scr[...] = _layer_norm(v_scr[...], g_ref[...], b_ref[...])
    row = lax.broadcasted_iota(jnp.int32, (CHUNK, CHUNK), 0)
    col = lax.broadcasted_iota(jnp.int32, (CHUNK, CHUNK), 1)
    causal = col <= row
    for g, gs in enumerate(groups):
        u = u_queue.pop(0)
        if g + ahead < A_GROUPS:
            u_queue.append(gelu_proj(groups[g + ahead]))
        wsg = jnp.where(causal, ws_ref[g], 0.0).astype(BF16)
        bsg = bs_ref[g]
        for c in range(tm // CHUNK):
            rs = slice(c * CHUNK, (c + 1) * CHUNK)
            s = jnp.dot(wsg, v_scr[rs, gs].astype(BF16), preferred_element_type=F32) + bsg
            y_ref[rs, gs] = (u[rs, :] * s).astype(BF16)
    _run_casts(cast_src, cast_dst)


def _sgu(x, w_in, b_in, g, b, w_s, b_s, layer, casts=()):
    S = x.shape[0]
    tm = TM_SGU
    nsteps = S // tm
    cast_specs = [_cast_specs(a, l, nsteps, lambda i: i) for a, l in casts]
    return pl.pallas_call(
        functools.partial(_sgu_kernel, len(casts)),
        out_shape=[jax.ShapeDtypeStruct((S, A_WIDTH), BF16)] + [c[2] for c in cast_specs],
        grid=(nsteps,),
        in_specs=[
            pl.BlockSpec((tm, D_MODEL), lambda i: (i, 0)),
            _resident(w_in.shape),
            _layer_slab((1, 2 * A_WIDTH), layer),
            _layer_slab((1, A_WIDTH), layer),
            _layer_slab((1, A_WIDTH), layer),
            _layer_slab((A_GROUPS, CHUNK, CHUNK), layer),
            _layer_slab((A_GROUPS, CHUNK, A_GROUP_DIM), layer),
        ] + [c[0] for c in cast_specs],
        out_specs=[pl.BlockSpec((tm, A_WIDTH), lambda i: (i, 0))] + [c[1] for c in cast_specs],
        scratch_shapes=[pltpu.VMEM((tm, A_WIDTH), F32)],
        compiler_params=_cparams("arbitrary"),
        name="sgu_mix",
    )(x, w_in, b_in, g, b, w_s, b_s, *[a for a, _ in casts])


def _proj_kernel(y_ref, w_ref, x_ref, g_ref, b_ref, o_ref):
    tm = x_ref.shape[0]
    chunks = [slice(r * ROWS_PROJ, (r + 1) * ROWS_PROJ) for r in range(tm // ROWS_PROJ)]
    for rs in chunks:
        o_ref[rs, :] = DEEPNORM_ALPHA * x_ref[rs, :] + jnp.dot(y_ref[rs, :], w_ref[...],
                                                               preferred_element_type=F32)
    for rs in chunks:
        o_ref[rs, :] = _layer_norm(o_ref[rs, :], g_ref[...], b_ref[...])


def _proj_res_ln(y, w, x, g, b, ln_row):
    S = x.shape[0]
    tm = TM_PROJ
    return pl.pallas_call(
        _proj_kernel,
        out_shape=jax.ShapeDtypeStruct((S, D_MODEL), F32),
        grid=(S // tm,),
        in_specs=[
            pl.BlockSpec((tm, y.shape[1]), lambda i: (i, 0)),
            _resident(w.shape),
            pl.BlockSpec((tm, D_MODEL), lambda i: (i, 0)),
            _layer_slab((1, D_MODEL), ln_row),
            _layer_slab((1, D_MODEL), ln_row),
        ],
        out_specs=pl.BlockSpec((tm, D_MODEL), lambda i: (i, 0)),
        compiler_params=_cparams("parallel"),
        name="proj_res_ln",
    )(y, w, x, g, b)


def _mlp_kernel(n_cast, x_ref, wu_ref, wd_ref, g_ref, b_ref, *rest):
    cast_src, o_ref, cast_dst = rest[:n_cast], rest[n_cast], rest[n_cast + 1:2 * n_cast + 1]
    xb_scr = rest[2 * n_cast + 1]
    j = pl.program_id(1)
    last = pl.num_programs(1) - 1

    def hidden():
        h = jnp.dot(xb_scr[...], wu_ref[...], preferred_element_type=F32)
        return jnp.square(jnp.maximum(h, 0.0)).astype(BF16)

    @pl.when(j == 0)
    def _():
        xb_scr[...] = x_ref[...].astype(BF16)
        o_ref[...] = DEEPNORM_ALPHA * x_ref[...] + jnp.dot(hidden(), wd_ref[...],
                                                           preferred_element_type=F32)
        _run_casts(cast_src, cast_dst)

    @pl.when(jnp.logical_and(j > 0, j < last))
    def _():
        o_ref[...] += jnp.dot(hidden(), wd_ref[...], preferred_element_type=F32)
        _run_casts(cast_src, cast_dst)

    @pl.when(j == last)
    def _():
        h = hidden()
        chunks = [slice(r * ROWS_LN, (r + 1) * ROWS_LN) for r in range(x_ref.shape[0] // ROWS_LN)]
        for rs in chunks:
            o_ref[rs, :] += jnp.dot(h[rs, :], wd_ref[...], preferred_element_type=F32)
        for rs in chunks:
            o_ref[rs, :] = _layer_norm(o_ref[rs, :], g_ref[...], b_ref[...])
        _run_casts(cast_src, cast_dst)


def _mlp(x, w_up, w_down, g, b, layer, casts=()):
    S = x.shape[0]
    tm, tf = TM_MLP, TF_MLP
    ni, nj = S // tm, D_FF // tf
    cast_specs = [_cast_specs(a, l, ni * nj, lambda i, j: i * nj + j) for a, l in casts]
    return pl.pallas_call(
        functools.partial(_mlp_kernel, len(casts)),
        out_shape=[jax.ShapeDtypeStruct((S, D_MODEL), F32)] + [c[2] for c in cast_specs],
        grid=(ni, nj),
        in_specs=[
            pl.BlockSpec((tm, D_MODEL), lambda i, j: (i, 0)),
            pl.BlockSpec((D_MODEL, tf), lambda i, j: (0, j)),
            pl.BlockSpec((tf, D_MODEL), lambda i, j: (j, 0)),
            _layer_slab((1, D_MODEL), 2 * layer + 1),
            _layer_slab((1, D_MODEL), 2 * layer + 1),
        ] + [c[0] for c in cast_specs],
        out_specs=[pl.BlockSpec((tm, D_MODEL), lambda i, j: (i, 0))] + [c[1] for c in cast_specs],
        scratch_shapes=[pltpu.VMEM((tm, D_MODEL), BF16)],
        compiler_params=_cparams("arbitrary", "arbitrary"),
        name="mlp_res_ln",
    )(x, w_up, w_down, g, b, *[a for a, _ in casts])


def _kv_kernel(x_ref, w_ref, b_ref, cos_ref, sin_ref, ek_ref, ev_ref, vone_ref, k_out, v_out):
    xb = x_ref[...].astype(BF16)
    kv = jnp.dot(xb, w_ref[...], preferred_element_type=F32) + b_ref[...]
    k1 = kv[:, :LANES]
    k2 = kv[:, LANES:2 * LANES]
    v = kv[:, KV_WIDTH:]
    c = cos_ref[...]
    s = sin_ref[...]
    kr = jnp.concatenate([k1 * c - k2 * s, k2 * c + k1 * s], axis=1).astype(BF16)
    kp = jnp.dot(kr, ek_ref[...], preferred_element_type=F32)
    vp = jnp.dot(v.astype(BF16), ev_ref[...], preferred_element_type=F32) + vone_ref[...]
    for h in range(N_KV_HEADS):
        for p in range(2):
            k0 = (h * 2 + p) * LANES
            k_out[h, p] = kp[:, k0:k0 + LANES].astype(BF16)
            v0 = (h * 2 + p) * 2 * LANES
            v_out[h, p] = vp[:, v0:v0 + 2 * LANES].astype(BF16)


def _kv_proj(x, w, b, cos_k, sin_k, ek, ev, vone):
    S = x.shape[0]
    tm = TM_KV
    return pl.pallas_call(
        _kv_kernel,
        out_shape=(jax.ShapeDtypeStruct((N_KV_HEADS, 2, S, LANES), BF16),
                   jax.ShapeDtypeStruct((N_KV_HEADS, 2, S, 2 * LANES), BF16)),
        grid=(S // tm,),
        in_specs=[
            pl.BlockSpec((tm, D_MODEL), lambda i: (i, 0)),
            _resident(w.shape),
            _resident(b.shape),
            pl.BlockSpec((tm, LANES), lambda i: (i, 0)),
            pl.BlockSpec((tm, LANES), lambda i: (i, 0)),
            _resident(ek.shape),
            _resident(ev.shape),
            _resident(vone.shape),
        ],
        out_specs=(pl.BlockSpec((N_KV_HEADS, 2, tm, LANES), lambda i: (0, 0, i, 0)),
                   pl.BlockSpec((N_KV_HEADS, 2, tm, 2 * LANES), lambda i: (0, 0, i, 0))),
        compiler_params=_cparams("parallel"),
        name="kv_proj",
    )(x, w, b, cos_k, sin_k, ek, ev, vone)


def _attn_kernel(sink_ref, x_ref, wq_ref, bq_ref, cos_ref, sin_ref, kp_ref, kc_ref, vp_ref,
                 vc_ref, bias_ref, eye_ref, o_ref, q_scr):
    tm = x_ref.shape[0]
    i = pl.program_id(0)
    xb = x_ref[...].astype(BF16)
    cos = cos_ref[...]
    sin = sin_ref[...]
    first_half = lax.broadcasted_iota(jnp.int32, (tm, LANES), 1) % HEAD_DIM < HALF
    ncol = 2 * LANES
    for c in range(D_MODEL // ncol):
        cs = slice(c * ncol, (c + 1) * ncol)
        q = jnp.dot(xb, wq_ref[:, cs], preferred_element_type=F32) + bq_ref[:, cs]
        for t in range(2):
            qt = q[:, t * LANES:(t + 1) * LANES]
            partner = jnp.where(first_half, pltpu.roll(qt, LANES - HALF, 1), pltpu.roll(qt, HALF, 1))
            qt = qt * cos + partner * sin
            q_scr[:, c * ncol + t * LANES:c * ncol + (t + 1) * LANES] = qt.astype(BF16)

    def band_rows(prev_ref, cur_ref, h, p, b):
        if b == 0:
            return jnp.concatenate([prev_ref[h, p], cur_ref[h, p, :WINDOW, :]], axis=0)
        return cur_ref[h, p, (b - 1) * WINDOW:(b + 1) * WINDOW, :]

    lane = lax.broadcasted_iota(jnp.int32, (WINDOW, LANES), 1)
    even_lanes = lane < HEAD_DIM
    first_tile = jnp.where(i == 0, 1, 0)
    nb = tm // WINDOW

    def stage_scores(b):
        rs = slice(b * WINDOW, (b + 1) * WINDOW)
        band = slice(b * WINDOW, b * WINDOW + BAND)
        mask_cols = bias_ref[first_tile] if b == 0 else bias_ref[0]
        scores = []
        for h in range(N_KV_HEADS):
            c0 = h * Q_PER_KV * HEAD_DIM
            lhs = jnp.concatenate(
                [q_scr[rs, c0 + j * LANES:c0 + (j + 1) * LANES] for j in range(PAIRS)], axis=0)
            kcat = jnp.concatenate([band_rows(kp_ref, kc_ref, h, 0, b),
                                    band_rows(kp_ref, kc_ref, h, 1, b)], axis=0)
            lhs = jnp.concatenate([lhs, eye_ref[...]], axis=1)
            kcat = jnp.concatenate([kcat, mask_cols], axis=1)
            scores.append(lax.dot_general(lhs, kcat, (((1,), (1,)), ((), ())),
                                          preferred_element_type=F32))
        return scores

    def stage_softmax(b, scores):
        pmats = []
        sinks = []
        for h in range(N_KV_HEADS):
            s = scores[h]
            p_rows = []
            sink_terms = []
            for j in range(PAIRS):
                p_cols = []
                sink_arg = []
                for p in range(2):
                    sub = s[j * WINDOW:(j + 1) * WINDOW, p * BAND:(p + 1) * BAND]
                    sk = sink_ref[h * Q_PER_KV + 2 * j + p]
                    m = jnp.maximum(jnp.max(sub, axis=1, keepdims=True), sk)
                    p_cols.append(jnp.exp2(sub - m).astype(BF16))
                    sink_arg.append(sk - m)
                p_rows.append(jnp.concatenate(p_cols, axis=1))
                sink_terms.append(jnp.exp2(jnp.where(even_lanes, sink_arg[0], sink_arg[1])))
            pmats.append(jnp.concatenate(p_rows, axis=0))
            sinks.append(sink_terms)
        return pmats, sinks

    def stage_values(b, pmats, sinks):
        rs = slice(b * WINDOW, (b + 1) * WINDOW)
        band = slice(b * WINDOW, b * WINDOW + BAND)
        for h in range(N_KV_HEADS):
            c0 = h * Q_PER_KV * HEAD_DIM
            vcat = jnp.concatenate([band_rows(vp_ref, vc_ref, h, 0, b),
                                    band_rows(vp_ref, vc_ref, h, 1, b)], axis=0)
            oa = jnp.dot(pmats[h], vcat, preferred_element_type=F32)
            for j in range(PAIRS):
                js = slice(j * WINDOW, (j + 1) * WINDOW)
                den = oa[js, LANES:] + sinks[h][j]
                o_ref[rs, c0 + j * LANES:c0 + (j + 1) * LANES] = (oa[js, :LANES] / den).astype(BF16)

    scores = stage_scores(0)
    for b in range(nb):
        nxt = stage_scores(b + 1) if b + 1 < nb else None
        pmats, sinks = stage_softmax(b, scores)
        stage_values(b, pmats, sinks)
        scores = nxt


def _attention(sinks, x, w_q, b_q, cos_q, sin_q, kmat, vmat, bias, eye, layer):
    S = x.shape[0]
    tm = TM_ATT
    nb = tm // WINDOW
    prev = lambda i, s: (0, 0, jnp.maximum(i * nb - 1, 0), 0)
    cur = lambda i, s: (0, 0, i, 0)
    grid_spec = pltpu.PrefetchScalarGridSpec(
        num_scalar_prefetch=1,
        grid=(S // tm,),
        in_specs=[
            pl.BlockSpec((tm, D_MODEL), lambda i, s: (i, 0)),
            _resident(w_q.shape),
            _layer_slab(b_q.shape[1:], layer),
            pl.BlockSpec((tm, LANES), lambda i, s: (i, 0)),
            pl.BlockSpec((tm, LANES), lambda i, s: (i, 0)),
            pl.BlockSpec((N_KV_HEADS, 2, WINDOW, LANES), prev),
            pl.BlockSpec((N_KV_HEADS, 2, tm, LANES), cur),
            pl.BlockSpec((N_KV_HEADS, 2, WINDOW, 2 * LANES), prev),
            pl.BlockSpec((N_KV_HEADS, 2, tm, 2 * LANES), cur),
            _resident(bias.shape),
            _resident(eye.shape),
        ],
        out_specs=pl.BlockSpec((tm, D_MODEL), lambda i, s: (i, 0)),
        scratch_shapes=[pltpu.VMEM((tm, D_MODEL), BF16)],
    )
    return pl.pallas_call(
        _attn_kernel,
        out_shape=jax.ShapeDtypeStruct((S, D_MODEL), BF16),
        grid_spec=grid_spec,
        compiler_params=_cparams("parallel"),
        name="swa_attention",
    )(sinks, x, w_q, b_q, cos_q, sin_q, kmat, kmat, vmat, vmat, bias, eye)


def _k_perm():
    perm = np.zeros(KV_WIDTH, np.int32)
    for half in range(2):
        for h in range(N_KV_HEADS):
            for d in range(HALF):
                perm[half * LANES + h * HALF + d] = h * HEAD_DIM + half * HALF + d
    return perm


def _placements():
    ek = np.zeros((KV_WIDTH, N_KV_HEADS * 2 * LANES), np.float32)
    ev = np.zeros((KV_WIDTH, N_KV_HEADS * 2 * 2 * LANES), np.float32)
    vone = np.zeros((1, N_KV_HEADS * 2 * 2 * LANES), np.float32)
    for h in range(N_KV_HEADS):
        for p in range(2):
            for half in range(2):
                for d in range(HALF):
                    ek[half * LANES + h * HALF + d,
                       (h * 2 + p) * LANES + p * HEAD_DIM + half * HALF + d] = 1.0
            for d in range(HEAD_DIM):
                base = (h * 2 + p) * 2 * LANES
                ev[h * HEAD_DIM + d, base + p * HEAD_DIM + d] = 1.0
                vone[0, base + LANES + p * HEAD_DIM + d] = 1.0
    return ek, ev, vone


def _band_mask():
    i = np.arange(WINDOW)[None, :]
    j = np.arange(BAND)[:, None]
    in_band = (j > i) & (j <= i + WINDOW)
    mask = np.zeros((2, BAND, WINDOW), np.float32)
    mask[0] = np.where(in_band, 0.0, NEG_BF16)
    mask[1] = np.where(in_band & (j >= WINDOW), 0.0, NEG_BF16)
    eye = np.tile(np.eye(WINDOW, dtype=np.float32), (PAIRS, 1))
    return np.tile(mask, (1, 2, 1)), eye


def kernel(x, a_w_in, a_b_in, a_ln_v_g, a_ln_v_b, a_w_s, a_b_s, a_w_out, kv_w, kv_b, b_w_q,
           b_b_q, b_sinks, b_w_o, mlp_w_up, mlp_w_down, ln_g, ln_b):
    B, S, _ = x.shape
    assert B == 1 and S % TM_MLP == 0
    h = x.reshape(S, D_MODEL)

    inv_freq = jnp.tile(ROPE_THETA ** (-jnp.arange(0, HEAD_DIM, 2, dtype=F32) / HEAD_DIM), 4)
    blk = (jnp.arange(S // WINDOW, dtype=jnp.int32) * WINDOW).astype(F32)[:, None] * inv_freq[None, :]
    off = jnp.arange(WINDOW, dtype=jnp.int32).astype(F32)[:, None] * inv_freq[None, :]
    cb, sb = jnp.cos(blk)[:, None, :], jnp.sin(blk)[:, None, :]
    co, so = jnp.cos(off)[None, :, :], jnp.sin(off)[None, :, :]
    cos_k = (cb * co - sb * so).reshape(S, LANES)
    sin_k = (sb * co + cb * so).reshape(S, LANES)
    scale = HEAD_DIM ** -0.5 * LOG2_E
    rot_sign = np.tile(np.repeat(np.array([-scale, scale], np.float32), HALF), 2)
    cos_q = cos_k * scale
    sin_q = sin_k * rot_sign[None, :]

    k_perm = _k_perm()
    ek, ev, vone = _placements()
    ek = jnp.asarray(ek, BF16)
    ev = jnp.asarray(ev, BF16)
    vone = jnp.asarray(vone, F32)
    mask, eye = _band_mask()
    mask = jnp.asarray(mask, BF16)
    eye = jnp.asarray(eye, BF16)

    rows = lambda v: v.reshape(-1, 1, v.shape[-1])
    b_in, ln_v_g, ln_v_b = rows(a_b_in), rows(a_ln_v_g), rows(a_ln_v_b)
    b_q = rows(b_b_q)
    g_rows, b_rows = rows(ln_g), rows(ln_b)
    bs = jnp.broadcast_to(a_b_s[:, :, :, None], (N_A_LAYERS, A_GROUPS, CHUNK, A_GROUP_DIM))

    def mixer_casts(layer):
        if layer < N_A_LAYERS:
            return [(a_w_in, layer), (a_w_out, layer)]
        return [(b_w_q, layer - N_A_LAYERS), (b_w_o, layer - N_A_LAYERS)]

    def layer_casts(layer):
        return mixer_casts(layer) + [(mlp_w_up, layer), (mlp_w_down, layer)]

    w_a = a_w_in[0].astype(BF16)
    for layer in range(DEPTH):
        nxt = layer_casts(layer + 1) if layer + 1 < DEPTH else []
        if layer < N_A_LAYERS:
            first = [(a_w_out, 0), (mlp_w_up, 0), (mlp_w_down, 0)] if layer == 0 else []
            y, *done = _sgu(h, w_a, b_in, ln_v_g, ln_v_b, a_w_s, bs, layer, first)
            if layer == 0:
                w_b, w_up, w_down = done
        else:
            if layer == N_A_LAYERS:
                w_kv = jnp.concatenate([kv_w[:, :KV_WIDTH][:, k_perm], kv_w[:, KV_WIDTH:]], axis=1)
                b_kv = jnp.concatenate([kv_b[:KV_WIDTH][k_perm], kv_b[KV_WIDTH:]])
                kmat, vmat = _kv_proj(h, w_kv.astype(BF16), b_kv.reshape(1, -1), cos_k, sin_k,
                                      ek, ev, vone)
            j = layer - N_A_LAYERS
            y = _attention(b_sinks[j] * LOG2_E, h, w_a, b_q, cos_q, sin_q, kmat, vmat, mask, eye, j)
        h = _proj_res_ln(y, w_b, h, g_rows, b_rows, 2 * layer)
        h, *done = _mlp(h, w_up, w_down, g_rows, b_rows, layer, nxt)
        if nxt:
            w_a, w_b, w_up, w_down = done
    return h.reshape(B, S, D_MODEL)
```
